```python
import jax, jax.numpy as jnp
from jax import lax
import numpy as np

D_MODEL = 1024
BATCH = 4
SEQ = 4096
DEPTH = 4
DEC_BATCH = 8
DEC_SEQ = 8192
PAST_LEN = 128

N_META = 16
GRID_W = 64
HEAD_DIM = 64
ATTN_WIDTH = 512
N_Q_HEADS = ATTN_WIDTH // HEAD_DIM
N_KV_HEADS = 2
KV_GROUP = N_Q_HEADS // N_KV_HEADS
KV_WIDTH = N_KV_HEADS * HEAD_DIM
CONV_WIDTH = D_MODEL - ATTN_WIDTH
CONV_GROUPS = CONV_WIDTH // HEAD_DIM
D_FF = 2816
Q_BLOCK = 128
ROPE_THETA = 10000.0
ROPE_PAIRS_AXIS = HEAD_DIM // 4
EPS = 1e-6
IN_WIDTH = ATTN_WIDTH + 2 * KV_WIDTH + 3 * CONV_WIDTH
SPLIT_POINTS = (ATTN_WIDTH,
                ATTN_WIDTH + KV_WIDTH,
                ATTN_WIDTH + 2 * KV_WIDTH,
                ATTN_WIDTH + 2 * KV_WIDTH + CONV_WIDTH,
                ATTN_WIDTH + 2 * KV_WIDTH + 2 * CONV_WIDTH)

kernel_name = "hymba_conv_axial_gqa_macaron_encoder"


def rms_norm(x, g):
    xf = x.astype(jnp.float32)
    y = xf * lax.rsqrt(jnp.mean(xf * xf, axis=-1, keepdims=True) + EPS)
    return (y * g.astype(jnp.float32)).astype(x.dtype)


def group_rms_norm(x, g, n_groups):
    b, l, w = x.shape
    xf = x.astype(jnp.float32).reshape(b, l, n_groups, w // n_groups)
    y = xf * lax.rsqrt(jnp.mean(xf * xf, axis=-1, keepdims=True) + EPS)
    return (y.reshape(b, l, w) * g.astype(jnp.float32)).astype(x.dtype)


def swiglu(x, w_gate, w_up, w_down):
    return (jax.nn.silu(x @ w_gate) * (x @ w_up)) @ w_down


def rope_tables(n_tokens):
    rows = n_tokens // GRID_W
    row = jnp.repeat(jnp.arange(rows, dtype=jnp.float32), GRID_W)
    col = jnp.tile(jnp.arange(GRID_W, dtype=jnp.float32), rows)
    row = jnp.concatenate([jnp.zeros((N_META,), jnp.float32), row])
    col = jnp.concatenate([jnp.zeros((N_META,), jnp.float32), col])
    freqs = ROPE_THETA ** (-jnp.arange(ROPE_PAIRS_AXIS, dtype=jnp.float32) / ROPE_PAIRS_AXIS)
    ang = jnp.concatenate([row[:, None] * freqs, col[:, None] * freqs], axis=-1)
    return jnp.cos(ang), jnp.sin(ang)


def apply_rope(x, cos, sin):
    b, l, h, d = x.shape
    xf = x.astype(jnp.float32).reshape(b, l, h, d // 2, 2)
    x0, x1 = xf[..., 0], xf[..., 1]
    c = cos[None, :, None, :]
    s = sin[None, :, None, :]
    out = jnp.stack([x0 * c - x1 * s, x0 * s + x1 * c], axis=-1)
    return out.reshape(b, l, h, d).astype(x.dtype)


def attend(qb, k, v):
    s = jnp.einsum("bqhgd,bkhd->bhgqk", qb, k, preferred_element_type=jnp.float32)
    p = jax.nn.softmax(s * (HEAD_DIM ** -0.5), axis=-1).astype(v.dtype)
    return jnp.einsum("bhgqk,bkhd->bqhgd", p, v)


def attention_group(a_q, a_k, a_v, q_gain, k_gain, cos, sin):
    b, l, _ = a_q.shape
    q = rms_norm(a_q.reshape(b, l, N_Q_HEADS, HEAD_DIM), q_gain)
    k = rms_norm(a_k.reshape(b, l, N_KV_HEADS, HEAD_DIM), k_gain)
    v = a_v.reshape(b, l, N_KV_HEADS, HEAD_DIM)
    q = apply_rope(q, cos, sin).reshape(b, l, N_KV_HEADS, KV_GROUP, HEAD_DIM)
    k = apply_rope(k, cos, sin)
    o_meta = attend(q[:, :N_META], k, v).reshape(b, N_META, ATTN_WIDTH)
    n = l - N_META
    n_blk = n // Q_BLOCK
    q_blocks = q[:, N_META:].reshape(b, n_blk, Q_BLOCK, N_KV_HEADS, KV_GROUP, HEAD_DIM)
    q_blocks = jnp.moveaxis(q_blocks, 1, 0)
    o = lax.map(lambda qb: attend(qb, k, v), q_blocks)
    o = jnp.moveaxis(o, 0, 1).reshape(b, n, ATTN_WIDTH)
    return jnp.concatenate([o_meta, o], axis=1)


def conv_group(c_b, c_c, c_h, w, bias):
    u = c_c * c_h
    up = jnp.pad(u, ((0, 0), (1, 1), (0, 0)))
    y = up[:, :-2] * w[0] + up[:, 1:-1] * w[1] + up[:, 2:] * w[2] + bias
    return c_b * y


def trunk(x, meta_tokens, ffn1_norm, ffn1_w_gate, ffn1_w_up, ffn1_w_down, mix_norm, w_in,
          q_norm, k_norm, conv_w, conv_b, attn_out_norm, conv_out_norm, w_out,
          ffn2_norm, ffn2_w_gate, ffn2_w_up, ffn2_w_down, final_norm):
    b, n, d = x.shape
    meta = jnp.broadcast_to(meta_tokens[None].astype(x.dtype), (b, N_META, d))
    h = jnp.concatenate([meta, x], axis=1)
    cos, sin = rope_tables(n)
    for l in range(DEPTH):
        h = h + 0.5 * swiglu(rms_norm(h, ffn1_norm[l]), ffn1_w_gate[l], ffn1_w_up[l], ffn1_w_down[l])
        u = rms_norm(h, mix_norm[l]) @ w_in[l]
        a_q, a_k, a_v, c_b, c_c, c_h = jnp.split(u, SPLIT_POINTS, axis=-1)
        y_att = attention_group(a_q, a_k, a_v, q_norm[l], k_norm[l], cos, sin)
        y_conv = conv_group(c_b, c_c, c_h, conv_w[l], conv_b[l])
        y_att = group_rms_norm(y_att, attn_out_norm[l], N_Q_HEADS)
        y_conv = group_rms_norm(y_conv, conv_out_norm[l], CONV_GROUPS)
        h = h + jnp.concatenate([y_att, y_conv], axis=-1) @ w_out[l]
        h = h + 0.5 * swiglu(rms_norm(h, ffn2_norm[l]), ffn2_w_gate[l], ffn2_w_up[l], ffn2_w_down[l])
    return rms_norm(h[:, N_META:], final_norm)


def setup_inputs(seed: int = 0) -> dict:
    key = jax.random.key(seed)
    ks = jax.random.split(key, 24)
    f32 = jnp.float32

    def nrm(k, shape, scale):
        return jax.random.normal(k, shape, f32) * scale

    def gain(k, shape):
        return 1.0 + 0.01 * jax.random.normal(k, shape, f32)

    return {
        "x_prompt": nrm(ks[0], (BATCH, SEQ, D_MODEL), 1.0),
        "x_sample": nrm(ks[1], (DEC_BATCH, DEC_SEQ, D_MODEL), 1.0),
        "meta_tokens": nrm(ks[2], (N_META, D_MODEL), 1.0),
        "ffn1_norm": gain(ks[3], (DEPTH, D_MODEL)),
        "ffn1_w_gate": nrm(ks[4], (DEPTH, D_MODEL, D_FF), D_MODEL ** -0.5),
        "ffn1_w_up": nrm(ks[5], (DEPTH, D_MODEL, D_FF), D_MODEL ** -0.5),
        "ffn1_w_down": nrm(ks[6], (DEPTH, D_FF, D_MODEL), D_FF ** -0.5),
        "mix_norm": gain(ks[7], (DEPTH, D_MODEL)),
        "w_in": nrm(ks[8], (DEPTH, D_MODEL, IN_WIDTH), D_MODEL ** -0.5),
        "q_norm": gain(ks[9], (DEPTH, HEAD_DIM)),
        "k_norm": gain(ks[10], (DEPTH, HEAD_DIM)),
        "conv_w": nrm(ks[11], (DEPTH, 3, CONV_WIDTH), 3 ** -0.5),
        "conv_b": nrm(ks[12], (DEPTH, CONV_WIDTH), 0.01),
        "attn_out_norm": gain(ks[13], (DEPTH, ATTN_WIDTH)),
        "conv_out_norm": gain(ks[14], (DEPTH, CONV_WIDTH)),
        "w_out": nrm(ks[15], (DEPTH, D_MODEL, D_MODEL), D_MODEL ** -0.5),
        "ffn2_norm": gain(ks[16], (DEPTH, D_MODEL)),
        "ffn2_w_gate": nrm(ks[17], (DEPTH, D_MODEL, D_FF), D_MODEL ** -0.5),
        "ffn2_w_up": nrm(ks[18], (DEPTH, D_MODEL, D_FF), D_MODEL ** -0.5),
        "ffn2_w_down": nrm(ks[19], (DEPTH, D_FF, D_MODEL), D_FF ** -0.5),
        "final_norm": gain(ks[20], (D_MODEL,)),
    }


def reference(x_prompt, x_sample, meta_tokens, ffn1_norm, ffn1_w_gate, ffn1_w_up, ffn1_w_down,
              mix_norm, w_in, q_norm, k_norm, conv_w, conv_b, attn_out_norm, conv_out_norm, w_out,
              ffn2_norm, ffn2_w_gate, ffn2_w_up, ffn2_w_down, final_norm):
    y_prompt = trunk(x_prompt, meta_tokens, ffn1_norm, ffn1_w_gate, ffn1_w_up, ffn1_w_down,
                     mix_norm, w_in, q_norm, k_norm, conv_w, conv_b, attn_out_norm, conv_out_norm,
                     w_out, ffn2_norm, ffn2_w_gate, ffn2_w_up, ffn2_w_down, final_norm)
    y_sample = trunk(x_sample, meta_tokens, ffn1_norm, ffn1_w_gate, ffn1_w_up, ffn1_w_down,
                     mix_norm, w_in, q_norm, k_norm, conv_w, conv_b, attn_out_norm, conv_out_norm,
                     w_out, ffn2_norm, ffn2_w_gate, ffn2_w_up, ffn2_w_down, final_norm)
    return (y_prompt, y_sample)
```

```python
import functools

import numpy as np
import jax
import jax.numpy as jnp
from jax import lax
from jax.experimental import pallas as pl
from jax.experimental.pallas import tpu as pltpu

F32 = jnp.float32
BF16 = jnp.bfloat16

D_MODEL = 1024
N_META = 16
GRID_W = 64
HEAD_DIM = 64
N_Q_HEADS = 8
N_KV_HEADS = 2
KV_GROUP = N_Q_HEADS // N_KV_HEADS
ATTN_WIDTH = N_Q_HEADS * HEAD_DIM
KV_WIDTH = N_KV_HEADS * HEAD_DIM
CONV_WIDTH = D_MODEL - ATTN_WIDTH
D_FF = 2816
ROPE_THETA = 10000.0
ROPE_PAIRS_AXIS = HEAD_DIM // 4
EPS = 1e-6
IN_WIDTH = ATTN_WIDTH + 2 * KV_WIDTH + 3 * CONV_WIDTH

LANES = 128
SUBLANES = 8
MXU_WIDTH = 256
SEQ_PAD = LANES
GROUP_LANES = KV_GROUP * HEAD_DIM
HALF_LANES = GROUP_LANES // 2
PAIRS = HEAD_DIM // 2
QK_COLS = 2 * N_KV_HEADS * GROUP_LANES
IN_COLS = QK_COLS + KV_WIDTH + 3 * CONV_WIDTH
KV_CHUNK = 4 * LANES
FF_CHUNKS = ((0, 768), (768, 768), (1536, 768), (2304, 512))
MASKED_SCORE = -1e30
VMEM_LIMIT = 56 * 1024 * 1024


def _const_spec(shape):
    zeros = (0,) * len(shape)
    return pl.BlockSpec(shape, lambda *_: zeros, pipeline_mode=pl.Buffered(1))


def _params(n_axes):
    return pltpu.CompilerParams(dimension_semantics=("arbitrary",) * n_axes, vmem_limit_bytes=VMEM_LIMIT)


def _rms(x):
    return x * lax.rsqrt(jnp.mean(x * x, axis=-1, keepdims=True) + EPS)


def _segment_sums(x, seg_ref):
    hi = x.astype(BF16)
    lo = (x - hi.astype(F32)).astype(BF16)
    seg = seg_ref[...]
    return (jnp.dot(hi, seg, preferred_element_type=F32) + jnp.dot(lo, seg, preferred_element_type=F32))


def _ffn_body(h_ref, g_ref, wg_ref, wu_ref, wd_ref, o_ref, act_scr):
    x = h_ref[...]
    xn = (_rms(x) * g_ref[...]).astype(BF16)
    for start, size in FF_CHUNKS:
        gate = jnp.dot(xn, wg_ref[:, start:start + size], preferred_element_type=F32)
        up = jnp.dot(xn, wu_ref[:, start:start + size], preferred_element_type=F32)
        silu = gate * (1.0 / (1.0 + jnp.exp(-gate)))
        act_scr[:, start:start + size] = (silu * up).astype(BF16)
    o_ref[...] = x + 0.5 * jnp.dot(act_scr[...], wd_ref[...], preferred_element_type=F32)


def _ffn(h, gain, w_gate, w_up, w_down, tm):
    rows = h.shape[0]
    return pl.pallas_call(
        _ffn_body,
        grid=(rows // tm,),
        in_specs=[
            pl.BlockSpec((tm, D_MODEL), lambda i: (i, 0)),
            _const_spec((1, D_MODEL)),
            _const_spec((D_MODEL, D_FF)),
            _const_spec((D_MODEL, D_FF)),
            _const_spec((D_FF, D_MODEL)),
        ],
        out_specs=pl.BlockSpec((tm, D_MODEL), lambda i: (i, 0)),
        out_shape=jax.ShapeDtypeStruct((rows, D_MODEL), F32),
        scratch_shapes=[pltpu.VMEM((tm, D_FF), BF16)],
        compiler_params=_params(1),
        name="ffn",
    )(h, gain, w_gate, w_up, w_down)


def _in_body(h_ref, g_ref, w_ref, qkg_ref, cos_ref, sin_ref, seg_ref,
             q_ref, kt_ref, v_ref, cb_ref, ucv_ref, *, tm):
    xn = (_rms(h_ref[...]) * g_ref[...]).astype(BF16)
    cos = cos_ref[...]
    sin = sin_ref[...]
    for c in range(2 * N_KV_HEADS):
        u = jnp.dot(xn, w_ref[:, c * GROUP_LANES:(c + 1) * GROUP_LANES], preferred_element_type=F32)
        x0 = u[:, :HALF_LANES]
        x1 = u[:, HALF_LANES:]
        ssq = _segment_sums(x0 * x0 + x1 * x1, seg_ref)
        inv = lax.rsqrt(ssq * (1.0 / HEAD_DIM) + EPS)
        a0 = x0 * inv * qkg_ref[c:c + 1, :HALF_LANES]
        a1 = x1 * inv * qkg_ref[c:c + 1, HALF_LANES:]
        o0 = a0 * cos - a1 * sin
        o1 = a0 * sin + a1 * cos
        if c < N_KV_HEADS:
            q_ref[c, :, :HALF_LANES] = o0.astype(BF16)
            q_ref[c, :, HALF_LANES:] = o1.astype(BF16)
        else:
            for j in range(tm // LANES):
                rows = slice(j * LANES, (j + 1) * LANES)
                kt_ref[c - N_KV_HEADS, j, :HALF_LANES, :] = o0[rows, :].T.astype(BF16)
                kt_ref[c - N_KV_HEADS, j, HALF_LANES:, :] = o1[rows, :].T.astype(BF16)
    v = jnp.dot(xn, w_ref[:, QK_COLS:QK_COLS + KV_WIDTH], preferred_element_type=F32)
    lane = lax.broadcasted_iota(jnp.int32, v.shape, 1)
    v_ref[0] = jnp.where(lane < HEAD_DIM, v, 1.0).astype(BF16)
    v_ref[1] = jnp.where(lane < HEAD_DIM, pltpu.roll(v, HEAD_DIM, 1), 1.0).astype(BF16)
    conv0 = QK_COLS + KV_WIDTH
    cb_ref[...] = jnp.dot(xn, w_ref[:, conv0:conv0 + CONV_WIDTH], preferred_element_type=F32)
    cc = jnp.dot(xn, w_ref[:, conv0 + CONV_WIDTH:conv0 + 2 * CONV_WIDTH], preferred_element_type=F32)
    ch = jnp.dot(xn, w_ref[:, conv0 + 2 * CONV_WIDTH:conv0 + 3 * CONV_WIDTH], preferred_element_type=F32)
    ucv_ref[...] = cc * ch


def _in_proj(h, gain, w_in, qk_gain, cos, sin, seg, lp, tm):
    rows = h.shape[0]
    tiles_per_seq = lp // tm
    return pl.pallas_call(
        functools.partial(_in_body, tm=tm),
        grid=(rows // tm,),
        in_specs=[
            pl.BlockSpec((tm, D_MODEL), lambda i: (i, 0)),
            _const_spec((1, D_MODEL)),
            _const_spec((D_MODEL, IN_COLS)),
            _const_spec((2 * N_KV_HEADS, GROUP_LANES)),
            pl.BlockSpec((tm, HALF_LANES), lambda i: (i % tiles_per_seq, 0)),
            pl.BlockSpec((tm, HALF_LANES), lambda i: (i % tiles_per_seq, 0)),
            _const_spec((HALF_LANES, HALF_LANES)),
        ],
        out_specs=[
            pl.BlockSpec((N_KV_HEADS, tm, GROUP_LANES), lambda i: (0, i, 0)),
            pl.BlockSpec((N_KV_HEADS, tm // LANES, GROUP_LANES, LANES), lambda i: (0, i, 0, 0)),
            pl.BlockSpec((N_KV_HEADS, tm, LANES), lambda i: (0, i, 0)),
            pl.BlockSpec((tm, CONV_WIDTH), lambda i: (i, 0)),
            pl.BlockSpec((tm, CONV_WIDTH), lambda i: (i, 0)),
        ],
        out_shape=[
            jax.ShapeDtypeStruct((N_KV_HEADS, rows, GROUP_LANES), BF16),
            jax.ShapeDtypeStruct((N_KV_HEADS, rows // LANES, GROUP_LANES, LANES), BF16),
            jax.ShapeDtypeStruct((N_KV_HEADS, rows, LANES), BF16),
            jax.ShapeDtypeStruct((rows, CONV_WIDTH), F32),
            jax.ShapeDtypeStruct((rows, CONV_WIDTH), F32),
        ],
        compiler_params=_params(1),
        name="in_proj",
    )(h, gain, w_in, qk_gain, cos, sin, seg)


def _attn_body(q_ref, kt_ref, v_ref, gain_ref, o_ref, qs_scr, m_scr, acc_scr, *, tq, n_steps):
    q = q_ref[0].astype(F32)
    lane = lax.broadcasted_iota(jnp.int32, q.shape, 1)
    head_of_lane = (lane // PAIRS) % KV_GROUP
    for r in range(KV_GROUP):
        qs_scr[r * tq:(r + 1) * tq, :] = jnp.where(head_of_lane == r, q, 0.0).astype(BF16)

    s = jnp.dot(qs_scr[...], kt_ref[0, 0], preferred_element_type=F32)
    key = lax.broadcasted_iota(jnp.int32, s.shape, 1)
    s = jnp.where(key >= SEQ_PAD - N_META, s, MASKED_SCORE)
    m = jnp.max(s, axis=1, keepdims=True)
    p = jnp.exp(s - m).astype(BF16)
    m_scr[...] = m
    acc_scr[...] = jnp.dot(p, v_ref[0, 0:SEQ_PAD, :], preferred_element_type=F32)

    def step(c, carry):
        kt4 = kt_ref[0, pl.ds(1 + c * (KV_CHUNK // LANES), KV_CHUNK // LANES)]
        kt = jnp.concatenate([kt4[j] for j in range(KV_CHUNK // LANES)], axis=1)
        s = jnp.dot(qs_scr[...], kt, preferred_element_type=F32)
        m_old = m_scr[...]
        m_new = jnp.maximum(m_old, jnp.max(s, axis=1, keepdims=True))
        alpha = jnp.exp(m_old - m_new)
        p = jnp.exp(s - m_new).astype(BF16)
        row0 = pl.multiple_of(SEQ_PAD + c * KV_CHUNK, LANES)
        pv = jnp.dot(p, v_ref[0, pl.ds(row0, KV_CHUNK), :], preferred_element_type=F32)
        acc_scr[...] = alpha * acc_scr[...] + pv
        m_scr[...] = m_new
        return carry

    lax.fori_loop(0, n_steps, step, 0)

    acc = acc_scr[...]
    lane = lax.broadcasted_iota(jnp.int32, acc.shape, 1)
    is_value = lane < HEAD_DIM
    o = jnp.where(is_value, acc / pltpu.roll(acc, HEAD_DIM, 1), 0.0)
    inv = lax.rsqrt(jnp.sum(o * o, axis=1, keepdims=True) * (1.0 / HEAD_DIM) + EPS)
    y = o * inv
    heads = [y[r * tq:(r + 1) * tq, :] * gain_ref[0, r:r + 1, :] for r in range(KV_GROUP)]
    is_value = lax.broadcasted_iota(jnp.int32, (tq, LANES), 1) < HEAD_DIM
    o_ref[0, :, :LANES] = jnp.where(is_value, heads[0], pltpu.roll(heads[1], HEAD_DIM, 1)).astype(BF16)
    o_ref[0, :, LANES:] = jnp.where(is_value, heads[2], pltpu.roll(heads[3], HEAD_DIM, 1)).astype(BF16)


def _attention(q, kt, v, gain, batch, lp, tq):
    rows = batch * lp
    q_tiles = lp // tq
    chunks = lp // LANES
    n_steps = (lp - SEQ_PAD) // KV_CHUNK
    return pl.pallas_call(
        functools.partial(_attn_body, tq=tq, n_steps=n_steps),
        grid=(batch, N_KV_HEADS, q_tiles),
        in_specs=[
            pl.BlockSpec((1, tq, GROUP_LANES), lambda b, g, i: (g, b * q_tiles + i, 0)),
            pl.BlockSpec((1, chunks, GROUP_LANES, LANES), lambda b, g, i: (g, b, 0, 0)),
            pl.BlockSpec((1, lp, LANES), lambda b, g, i: (g, b, 0)),
            pl.BlockSpec((1, KV_GROUP, LANES), lambda b, g, i: (g, 0, 0)),
        ],
        out_specs=pl.BlockSpec((1, tq, GROUP_LANES), lambda b, g, i: (g, b * q_tiles + i, 0)),
        out_shape=jax.ShapeDtypeStruct((N_KV_HEADS, rows, GROUP_LANES), BF16),
        scratch_shapes=[
            pltpu.VMEM((KV_GROUP * tq, GROUP_LANES), BF16),
            pltpu.VMEM((KV_GROUP * tq, 1), F32),
            pltpu.VMEM((KV_GROUP * tq, LANES), F32),
        ],
        compiler_params=_params(3),
        name="attn",
    )(q, kt, v, gain)


def _out_body(ya_ref, cb_ref, ucv_ref, prev_ref, next_ref, h_ref, cw_ref, cbias_ref, cgain_ref,
              wo_ref, seg_ref, o_ref, ext_scr, *, tm, lp):
    ext_scr[0:SUBLANES, :] = prev_ref[...]
    ext_scr[SUBLANES:SUBLANES + tm, :] = ucv_ref[...]
    ext_scr[SUBLANES + tm:, :] = next_ref[...]
    pos = (pl.program_id(0) % (lp // tm)) * tm + lax.broadcasted_iota(jnp.int32, (tm, 1), 0)
    left = jnp.where(pos == SEQ_PAD - N_META, 0.0, ext_scr[SUBLANES - 1:SUBLANES - 1 + tm, :])
    right = jnp.where(pos == lp - 1, 0.0, ext_scr[SUBLANES + 1:SUBLANES + 1 + tm, :])
    conv = left * cw_ref[0:1, :] + ucv_ref[...] * cw_ref[1:2, :] + right * cw_ref[2:3, :] + cbias_ref[...]
    y = cb_ref[...] * conv
    acc = h_ref[...]
    for g in range(N_KV_HEADS):
        acc = acc + jnp.dot(ya_ref[g], wo_ref[g * GROUP_LANES:(g + 1) * GROUP_LANES, :],
                            preferred_element_type=F32)
    for half in range(CONV_WIDTH // MXU_WIDTH):
        cols = slice(half * MXU_WIDTH, (half + 1) * MXU_WIDTH)
        yh = y[:, cols]
        inv = lax.rsqrt(_segment_sums(yh * yh, seg_ref) * (1.0 / HEAD_DIM) + EPS)
        yn = (yh * inv * cgain_ref[:, cols]).astype(BF16)
        acc = acc + jnp.dot(yn, wo_ref[ATTN_WIDTH + half * MXU_WIDTH:ATTN_WIDTH + (half + 1) * MXU_WIDTH, :],
                            preferred_element_type=F32)
    o_ref[...] = acc


def _out_proj(ya, cb, ucv, h, conv_w, conv_b, conv_gain, w_out, seg, lp, tm):
    rows = h.shape[0]
    halo_per_tile = tm // SUBLANES
    last_halo = rows // SUBLANES - 1
    return pl.pallas_call(
        functools.partial(_out_body, tm=tm, lp=lp),
        grid=(rows // tm,),
        in_specs=[
            pl.BlockSpec((N_KV_HEADS, tm, GROUP_LANES), lambda i: (0, i, 0)),
            pl.BlockSpec((tm, CONV_WIDTH), lambda i: (i, 0)),
            pl.BlockSpec((tm, CONV_WIDTH), lambda i: (i, 0)),
            pl.BlockSpec((SUBLANES, CONV_WIDTH), lambda i: (jnp.maximum(i * halo_per_tile - 1, 0), 0)),
            pl.BlockSpec((SUBLANES, CONV_WIDTH), lambda i: (jnp.minimum((i + 1) * halo_per_tile, last_halo), 0)),
            pl.BlockSpec((tm, D_MODEL), lambda i: (i, 0)),
            _const_spec((3, CONV_WIDTH)),
            _const_spec((1, CONV_WIDTH)),
            _const_spec((1, CONV_WIDTH)),
            _const_spec((D_MODEL, D_MODEL)),
            _const_spec((MXU_WIDTH, MXU_WIDTH)),
        ],
        out_specs=pl.BlockSpec((tm, D_MODEL), lambda i: (i, 0)),
        out_shape=jax.ShapeDtypeStruct((rows, D_MODEL), F32),
        scratch_shapes=[pltpu.VMEM((tm + 2 * SUBLANES, CONV_WIDTH), F32)],
        compiler_params=_params(1),
        name="out_proj",
    )(ya, cb, ucv, ucv, ucv, h, conv_w, conv_b, conv_gain, w_out, seg)


def _final_body(h_ref, g_ref, o_ref):
    o_ref[0] = _rms(h_ref[...]) * g_ref[...]


def _final_norm(h, gain, batch, n, tm):
    lp = SEQ_PAD + n
    tiles = n // tm
    return pl.pallas_call(
        _final_body,
        grid=(batch, tiles),
        in_specs=[
            pl.BlockSpec((tm, D_MODEL), lambda b, i: (b * (lp // tm) + SEQ_PAD // tm + i, 0)),
            _const_spec((1, D_MODEL)),
        ],
        out_specs=pl.BlockSpec((1, tm, D_MODEL), lambda b, i: (b, i, 0)),
        out_shape=jax.ShapeDtypeStruct((batch, n, D_MODEL), F32),
        compiler_params=_params(2),
        name="final_norm",
    )(h, gain)


def _largest_tile(extent, cap):
    units = extent // LANES
    best = max(d for d in range(1, units + 1) if units % d == 0 and d * LANES <= cap)
    return best * LANES


def _in_column_order():
    halves_heads_pairs = [(half, r, i) for half in range(2) for r in range(KV_GROUP) for i in range(PAIRS)]
    cols = []
    for g in range(N_KV_HEADS):
        cols += [(KV_GROUP * g + r) * HEAD_DIM + 2 * i + half for half, r, i in halves_heads_pairs]
    for g in range(N_KV_HEADS):
        cols += [ATTN_WIDTH + g * HEAD_DIM + 2 * i + half for half, r, i in halves_heads_pairs]
    cols += list(range(ATTN_WIDTH + KV_WIDTH, IN_WIDTH))
    return np.asarray(cols, np.int32)


def _head_gain_order():
    return np.asarray([2 * i + half for half in range(2) for r in range(KV_GROUP) for i in range(PAIRS)], np.int32)


def _rope_tables(n):
    rows = n // GRID_W
    row = jnp.repeat(jnp.arange(rows, dtype=F32), GRID_W)
    col = jnp.tile(jnp.arange(GRID_W, dtype=F32), rows)
    row = jnp.concatenate([jnp.zeros((SEQ_PAD,), F32), row])
    col = jnp.concatenate([jnp.zeros((SEQ_PAD,), F32), col])
    freqs = ROPE_THETA ** (-jnp.arange(ROPE_PAIRS_AXIS, dtype=F32) / ROPE_PAIRS_AXIS)
    ang = jnp.concatenate([row[:, None] * freqs, col[:, None] * freqs], axis=-1)
    return jnp.tile(jnp.cos(ang), (1, KV_GROUP)), jnp.tile(jnp.sin(ang), (1, KV_GROUP))


def _block_diag_ones(size, block):
    idx = np.arange(size) // block
    return jnp.asarray(idx[:, None] == idx[None, :], BF16)


def _trunk(x, meta_tokens, p):
    batch, n, d = x.shape
    assert d == D_MODEL and n % KV_CHUNK == 0 and n % GRID_W == 0
    lp = SEQ_PAD + n
    rows = batch * lp
    tm_ffn = _largest_tile(rows, 512)
    tm_seq = _largest_tile(lp, 768)
    tq = LANES
    h = jnp.concatenate([
        jnp.zeros((batch, SEQ_PAD - N_META, d), F32),
        jnp.broadcast_to(meta_tokens[None].astype(F32), (batch, N_META, d)),
        x.astype(F32)], axis=1).reshape(rows, d)
    cos, sin = _rope_tables(n)
    for l in range(p["depth"]):
        h = _ffn(h, p["ffn1_norm"][l], p["ffn1_w_gate"][l], p["ffn1_w_up"][l], p["ffn1_w_down"][l], tm_ffn)
        q, kt, v, cb, ucv = _in_proj(h, p["mix_norm"][l], p["w_in"][l], p["qk_gain"][l], cos, sin,
                                     p["seg_pairs"], lp, tm_seq)
        ya = _attention(q, kt, v, p["attn_gain"][l], batch, lp, tq)
        h = _out_proj(ya, cb, ucv, h, p["conv_w"][l], p["conv_b"][l], p["conv_gain"][l], p["w_out"][l],
                      p["seg_heads"], lp, tm_seq)
        h = _ffn(h, p["ffn2_norm"][l], p["ffn2_w_gate"][l], p["ffn2_w_up"][l], p["ffn2_w_down"][l], tm_ffn)
    return _final_norm(h, p["final_norm"], batch, n, LANES)


def kernel(x_prompt, x_sample, meta_tokens, ffn1_norm, ffn1_w_gate, ffn1_w_up, ffn1_w_down, mix_norm, w_in, q_norm, k_norm, conv_w, conv_b, attn_out_norm, conv_out_norm, w_out, ffn2_norm, ffn2_w_gate, ffn2_w_up, ffn2_w_down, final_norm):
    depth = w_in.shape[0]
    order = _head_gain_order()
    q_gain = q_norm[:, order] * (HEAD_DIM ** -0.5)
    k_gain = k_norm[:, order]
    p = {
        "depth": depth,
        "ffn1_norm": ffn1_norm[:, None, :],
        "ffn1_w_gate": ffn1_w_gate.astype(BF16),
        "ffn1_w_up": ffn1_w_up.astype(BF16),
        "ffn1_w_down": ffn1_w_down.astype(BF16),
        "mix_norm": mix_norm[:, None, :],
        "w_in": w_in[:, :, _in_column_order()].astype(BF16),
        "qk_gain": jnp.stack([q_gain, q_gain, k_gain, k_gain], axis=1),
        "attn_gain": jnp.pad(attn_out_norm.reshape(depth, N_KV_HEADS, KV_GROUP, HEAD_DIM),
                             ((0, 0), (0, 0), (0, 0), (0, LANES - HEAD_DIM))).reshape(depth, N_KV_HEADS, KV_GROUP, LANES),
        "conv_w": conv_w,
        "conv_b": conv_b[:, None, :],
        "conv_gain": conv_out_norm[:, None, :],
        "w_out": w_out.astype(BF16),
        "ffn2_norm": ffn2_norm[:, None, :],
        "ffn2_w_gate": ffn2_w_gate.astype(BF16),
        "ffn2_w_up": ffn2_w_up.astype(BF16),
        "ffn2_w_down": ffn2_w_down.astype(BF16),
        "final_norm": final_norm[None, :],
        "seg_pairs": _block_diag_ones(HALF_LANES, PAIRS),
        "seg_heads": _block_diag_ones(MXU_WIDTH, HEAD_DIM),
    }
    return (_trunk(x_prompt, meta_tokens, p), _trunk(x_sample, meta_tokens, p))
```

```python
import functools

import numpy as np
import jax
import jax.numpy as jnp
from jax import lax
from jax.experimental import pallas as pl
from jax.experimental.pallas import tpu as pltpu

F32 = jnp.float32
BF16 = jnp.bfloat16

D_MODEL = 1024
N_META = 16
GRID_W = 64
HEAD_DIM = 64
N_Q_HEADS = 8
N_KV_HEADS = 2
KV_GROUP = N_Q_HEADS // N_KV_HEADS
ATTN_WIDTH = N_Q_HEADS * HEAD_DIM
KV_WIDTH = N_KV_HEADS * HEAD_DIM
CONV_WIDTH = D_MODEL - ATTN_WIDTH
D_FF = 2816
ROPE_THETA = 10000.0
ROPE_PAIRS_AXIS = HEAD_DIM // 4
EPS = 1e-6
IN_WIDTH = ATTN_WIDTH + 2 * KV_WIDTH + 3 * CONV_WIDTH

LANES = 128
SUBLANES = 8
MXU_WIDTH = 256
SEQ_PAD = LANES
GROUP_LANES = KV_GROUP * HEAD_DIM
HALF_LANES = GROUP_LANES // 2
PAIRS = HEAD_DIM // 2
QK_COLS = 2 * N_KV_HEADS * GROUP_LANES
IN_COLS = QK_COLS + KV_WIDTH + 3 * CONV_WIDTH
KV_CHUNK = 4 * LANES
FF_CHUNKS = ((0, 768), (768, 768), (1536, 768), (2304, 512))
MASKED_SCORE = -1e30
LOG2_E = 1.4426950408889634
VMEM_LIMIT = 56 * 1024 * 1024


def _const_spec(shape):
    zeros = (0,) * len(shape)
    return pl.BlockSpec(shape, lambda *_: zeros, pipeline_mode=pl.Buffered(1))


def _params(n_axes):
    return pltpu.CompilerParams(dimension_semantics=("arbitrary",) * n_axes, vmem_limit_bytes=VMEM_LIMIT)


def _rms(x):
    return x * lax.rsqrt(jnp.mean(x * x, axis=-1, keepdims=True) + EPS)


def _segment_sums(x, seg_ref):
    hi = x.astype(BF16)
    lo = (x - hi.astype(F32)).astype(BF16)
    seg = seg_ref[...]
    return (jnp.dot(hi, seg, preferred_element_type=F32) + jnp.dot(lo, seg, preferred_element_type=F32))


def _ffn_body(h_ref, g_ref, wg_ref, wu_ref, wd_ref, o_ref, act_scr):
    x = h_ref[...]
    xn = (_rms(x) * g_ref[...]).astype(BF16)
    for start, size in FF_CHUNKS:
        gate = jnp.dot(xn, wg_ref[:, start:start + size], preferred_element_type=F32)
        up = jnp.dot(xn, wu_ref[:, start:start + size], preferred_element_type=F32)
        silu = gate * (1.0 / (1.0 + jnp.exp(-gate)))
        act_scr[:, start:start + size] = (silu * up).astype(BF16)
    o_ref[...] = x + 0.5 * jnp.dot(act_scr[...], wd_ref[...], preferred_element_type=F32)


def _ffn(h, gain, w_gate, w_up, w_down, tm):
    rows = h.shape[0]
    return pl.pallas_call(
        _ffn_body,
        grid=(rows // tm,),
        in_specs=[
            pl.BlockSpec((tm, D_MODEL), lambda i: (i, 0)),
            _const_spec((1, D_MODEL)),
            _const_spec((D_MODEL, D_FF)),
            _const_spec((D_MODEL, D_FF)),
            _const_spec((D_FF, D_MODEL)),
        ],
        out_specs=pl.BlockSpec((tm, D_MODEL), lambda i: (i, 0)),
        out_shape=jax.ShapeDtypeStruct((rows, D_MODEL), F32),
        scratch_shapes=[pltpu.VMEM((tm, D_FF), BF16)],
        compiler_params=_params(1),
        name="ffn",
    )(h, gain, w_gate, w_up, w_down)


def _in_body(h_ref, g_ref, w_ref, qkg_ref, cos_ref, sin_ref, seg_ref,
             q_ref, kt_ref, v_ref, cb_ref, ucv_ref, *, tm):
    xn = (_rms(h_ref[...]) * g_ref[...]).astype(BF16)
    cos = cos_ref[...]
    sin = sin_ref[...]
    for c in range(2 * N_KV_HEADS):
        u = jnp.dot(xn, w_ref[:, c * GROUP_LANES:(c + 1) * GROUP_LANES], preferred_element_type=F32)
        x0 = u[:, :HALF_LANES]
        x1 = u[:, HALF_LANES:]
        ssq = _segment_sums(x0 * x0 + x1 * x1, seg_ref)
        inv = lax.rsqrt(ssq * (1.0 / HEAD_DIM) + EPS)
        a0 = x0 * inv * qkg_ref[c:c + 1, :HALF_LANES]
        a1 = x1 * inv * qkg_ref[c:c + 1, HALF_LANES:]
        o0 = a0 * cos - a1 * sin
        o1 = a0 * sin + a1 * cos
        if c < N_KV_HEADS:
            q_ref[c, :, :HALF_LANES] = o0.astype(BF16)
            q_ref[c, :, HALF_LANES:] = o1.astype(BF16)
        else:
            for j in range(tm // LANES):
                rows = slice(j * LANES, (j + 1) * LANES)
                kt_ref[c - N_KV_HEADS, j, :HALF_LANES, :] = o0[rows, :].T.astype(BF16)
                kt_ref[c - N_KV_HEADS, j, HALF_LANES:, :] = o1[rows, :].T.astype(BF16)
    v = jnp.dot(xn, w_ref[:, QK_COLS:QK_COLS + KV_WIDTH], preferred_element_type=F32)
    lane = lax.broadcasted_iota(jnp.int32, v.shape, 1)
    v_ref[0] = jnp.where(lane < HEAD_DIM, v, 1.0).astype(BF16)
    v_ref[1] = jnp.where(lane < HEAD_DIM, pltpu.roll(v, HEAD_DIM, 1), 1.0).astype(BF16)
    conv0 = QK_COLS + KV_WIDTH
    cb_ref[...] = jnp.dot(xn, w_ref[:, conv0:conv0 + CONV_WIDTH], preferred_element_type=F32)
    cc = jnp.dot(xn, w_ref[:, conv0 + CONV_WIDTH:conv0 + 2 * CONV_WIDTH], preferred_element_type=F32)
    ch = jnp.dot(xn, w_ref[:, conv0 + 2 * CONV_WIDTH:conv0 + 3 * CONV_WIDTH], preferred_element_type=F32)
    ucv_ref[...] = cc * ch


def _in_proj(h, gain, w_in, qk_gain, cos, sin, seg, lp, tm):
    rows = h.shape[0]
    tiles_per_seq = lp // tm
    return pl.pallas_call(
        functools.partial(_in_body, tm=tm),
        grid=(rows // tm,),
        in_specs=[
            pl.BlockSpec((tm, D_MODEL), lambda i: (i, 0)),
            _const_spec((1, D_MODEL)),
            _const_spec((D_MODEL, IN_COLS)),
            _const_spec((2 * N_KV_HEADS, GROUP_LANES)),
            pl.BlockSpec((tm, HALF_LANES), lambda i: (i % tiles_per_seq, 0)),
            pl.BlockSpec((tm, HALF_LANES), lambda i: (i % tiles_per_seq, 0)),
            _const_spec((HALF_LANES, HALF_LANES)),
        ],
        out_specs=[
            pl.BlockSpec((N_KV_HEADS, tm, GROUP_LANES), lambda i: (0, i, 0)),
            pl.BlockSpec((N_KV_HEADS, tm // LANES, GROUP_LANES, LANES), lambda i: (0, i, 0, 0)),
            pl.BlockSpec((N_KV_HEADS, tm, LANES), lambda i: (0, i, 0)),
            pl.BlockSpec((tm, CONV_WIDTH), lambda i: (i, 0)),
            pl.BlockSpec((tm, CONV_WIDTH), lambda i: (i, 0)),
        ],
        out_shape=[
            jax.ShapeDtypeStruct((N_KV_HEADS, rows, GROUP_LANES), BF16),
            jax.ShapeDtypeStruct((N_KV_HEADS, rows // LANES, GROUP_LANES, LANES), BF16),
            jax.ShapeDtypeStruct((N_KV_HEADS, rows, LANES), BF16),
            jax.ShapeDtypeStruct((rows, CONV_WIDTH), F32),
            jax.ShapeDtypeStruct((rows, CONV_WIDTH), F32),
        ],
        compiler_params=_params(1),
        name="in_proj",
    )(h, gain, w_in, qk_gain, cos, sin, seg)


def _attn_body(q_ref, kt_ref, v_ref, gain_ref, o_ref, qs_scr, s_scr, mx_scr, m_scr, acc_scr, *, tq, n_steps):
    q = q_ref[0].astype(F32)
    lane = lax.broadcasted_iota(jnp.int32, q.shape, 1)
    head_of_lane = (lane // PAIRS) % KV_GROUP
    for r in range(KV_GROUP):
        qs_scr[r * tq:(r + 1) * tq, :] = jnp.where(head_of_lane == r, q, 0.0).astype(BF16)

    def lane_tiles(x):
        return [x[:, j * LANES:(j + 1) * LANES] for j in range(x.shape[1] // LANES)]

    def produce(c, slot):
        kt4 = kt_ref[0, pl.ds(1 + c * (KV_CHUNK // LANES), KV_CHUNK // LANES)]
        kt = jnp.concatenate([kt4[j] for j in range(KV_CHUNK // LANES)], axis=1)
        s = jnp.dot(qs_scr[...], kt, preferred_element_type=F32)
        s_scr[slot] = s
        mx_scr[slot] = functools.reduce(jnp.maximum, lane_tiles(s))

    def accumulate(s_tiles, m_cur, v, first):
        if first:
            m_new = jnp.broadcast_to(m_cur, (m_cur.shape[0], LANES))
        else:
            m_old = m_scr[...]
            m_new = jnp.maximum(m_old, m_cur)
        p = jnp.concatenate([jnp.exp2(t - m_new) for t in s_tiles], axis=1).astype(BF16)
        pv = jnp.dot(p, v, preferred_element_type=F32)
        acc_scr[...] = pv if first else jnp.exp2(m_old - m_new) * acc_scr[...] + pv
        m_scr[...] = m_new

    def consume(c, slot):
        v = v_ref[0, pl.ds(pl.multiple_of(SEQ_PAD + c * KV_CHUNK, LANES), KV_CHUNK), :]
        accumulate(lane_tiles(s_scr[slot]), jnp.max(mx_scr[slot], axis=1, keepdims=True), v, False)

    s = jnp.dot(qs_scr[...], kt_ref[0, 0], preferred_element_type=F32)
    key = lax.broadcasted_iota(jnp.int32, s.shape, 1)
    s = jnp.where(key >= SEQ_PAD - N_META, s, MASKED_SCORE)
    accumulate([s], jnp.max(s, axis=1, keepdims=True), v_ref[0, 0:SEQ_PAD, :], True)

    produce(0, 0)

    def pair(i, carry):
        c = 2 * i
        produce(c + 1, 1)
        consume(c, 0)
        produce(c + 2, 0)
        consume(c + 1, 1)
        return carry

    lax.fori_loop(0, n_steps // 2 - 1, pair, 0)
    produce(n_steps - 1, 1)
    consume(n_steps - 2, 0)
    consume(n_steps - 1, 1)

    acc = acc_scr[...]
    is_value = lax.broadcasted_iota(jnp.int32, acc.shape, 1) < HEAD_DIM
    den = jnp.sum(jnp.where(is_value, 0.0, acc), axis=1, keepdims=True) * (1.0 / (LANES - HEAD_DIM))
    num_sq = jnp.sum(jnp.where(is_value, acc * acc, 0.0), axis=1, keepdims=True)
    r_den = 1.0 / den
    inv = lax.rsqrt(num_sq * (r_den * r_den) * (1.0 / HEAD_DIM) + EPS)
    y = acc * (r_den * inv)
    heads = [y[r * tq:(r + 1) * tq, :] * gain_ref[0, r:r + 1, :] for r in range(KV_GROUP)]
    is_value = lax.broadcasted_iota(jnp.int32, (tq, LANES), 1) < HEAD_DIM
    o_ref[0, :, :LANES] = jnp.where(is_value, heads[0], pltpu.roll(heads[1], HEAD_DIM, 1)).astype(BF16)
    o_ref[0, :, LANES:] = jnp.where(is_value, heads[2], pltpu.roll(heads[3], HEAD_DIM, 1)).astype(BF16)


def _attention(q, kt, v, gain, batch, lp, tq):
    rows = batch * lp
    q_tiles = lp // tq
    chunks = lp // LANES
    n_steps = (lp - SEQ_PAD) // KV_CHUNK
    return pl.pallas_call(
        functools.partial(_attn_body, tq=tq, n_steps=n_steps),
        grid=(batch, N_KV_HEADS, q_tiles),
        in_specs=[
            pl.BlockSpec((1, tq, GROUP_LANES), lambda b, g, i: (g, b * q_tiles + i, 0)),
            pl.BlockSpec((1, chunks, GROUP_LANES, LANES), lambda b, g, i: (g, b, 0, 0)),
            pl.BlockSpec((1, lp, LANES), lambda b, g, i: (g, b, 0)),
            pl.BlockSpec((1, KV_GROUP, LANES), lambda b, g, i: (g, 0, 0)),
        ],
        out_specs=pl.BlockSpec((1, tq, GROUP_LANES), lambda b, g, i: (g, b * q_tiles + i, 0)),
        out_shape=jax.ShapeDtypeStruct((N_KV_HEADS, rows, GROUP_LANES), BF16),
        scratch_shapes=[
            pltpu.VMEM((KV_GROUP * tq, GROUP_LANES), BF16),
            pltpu.VMEM((2, KV_GROUP * tq, KV_CHUNK), F32),
            pltpu.VMEM((2, KV_GROUP * tq, LANES), F32),
            pltpu.VMEM((KV_GROUP * tq, LANES), F32),
            pltpu.VMEM((KV_GROUP * tq, LANES), F32),
        ],
        compiler_params=_params(3),
        name="attn",
    )(q, kt, v, gain)


def _out_body(ya_ref, cb_ref, ucv_ref, prev_ref, next_ref, h_ref, cw_ref, cbias_ref, cgain_ref,
              wo_ref, seg_ref, o_ref, ext_scr, *, tm, lp):
    ext_scr[0:SUBLANES, :] = prev_ref[...]
    ext_scr[SUBLANES:SUBLANES + tm, :] = ucv_ref[...]
    ext_scr[SUBLANES + tm:, :] = next_ref[...]
    pos = (pl.program_id(0) % (lp // tm)) * tm + lax.broadcasted_iota(jnp.int32, (tm, 1), 0)
    left = jnp.where(pos == SEQ_PAD - N_META, 0.0, ext_scr[SUBLANES - 1:SUBLANES - 1 + tm, :])
    right = jnp.where(pos == lp - 1, 0.0, ext_scr[SUBLANES + 1:SUBLANES + 1 + tm, :])
    conv = left * cw_ref[0:1, :] + ucv_ref[...] * cw_ref[1:2, :] + right * cw_ref[2:3, :] + cbias_ref[...]
    y = cb_ref[...] * conv
    acc = h_ref[...]
    for g in range(N_KV_HEADS):
        acc = acc + jnp.dot(ya_ref[g], wo_ref[g * GROUP_LANES:(g + 1) * GROUP_LANES, :],
                            preferred_element_type=F32)
    for half in range(CONV_WIDTH // MXU_WIDTH):
        cols = slice(half * MXU_WIDTH, (half + 1) * MXU_WIDTH)
        yh = y[:, cols]
        inv = lax.rsqrt(_segment_sums(yh * yh, seg_ref) * (1.0 / HEAD_DIM) + EPS)
        yn = (yh * inv * cgain_ref[:, cols]).astype(BF16)
        acc = acc + jnp.dot(yn, wo_ref[ATTN_WIDTH + half * MXU_WIDTH:ATTN_WIDTH + (half + 1) * MXU_WIDTH, :],
                            preferred_element_type=F32)
    o_ref[...] = acc


def _out_proj(ya, cb, ucv, h, conv_w, conv_b, conv_gain, w_out, seg, lp, tm):
    rows = h.shape[0]
    halo_per_tile = tm // SUBLANES
    last_halo = rows // SUBLANES - 1
    return pl.pallas_call(
        functools.partial(_out_body, tm=tm, lp=lp),
        grid=(rows // tm,),
        in_specs=[
            pl.BlockSpec((N_KV_HEADS, tm, GROUP_LANES), lambda i: (0, i, 0)),
            pl.BlockSpec((tm, CONV_WIDTH), lambda i: (i, 0)),
            pl.BlockSpec((tm, CONV_WIDTH), lambda i: (i, 0)),
            pl.BlockSpec((SUBLANES, CONV_WIDTH), lambda i: (jnp.maximum(i * halo_per_tile - 1, 0), 0)),
            pl.BlockSpec((SUBLANES, CONV_WIDTH), lambda i: (jnp.minimum((i + 1) * halo_per_tile, last_halo), 0)),
            pl.BlockSpec((tm, D_MODEL), lambda i: (i, 0)),
            _const_spec((3, CONV_WIDTH)),
            _const_spec((1, CONV_WIDTH)),
            _const_spec((1, CONV_WIDTH)),
            _const_spec((D_MODEL, D_MODEL)),
            _const_spec((MXU_WIDTH, MXU_WIDTH)),
        ],
        out_specs=pl.BlockSpec((tm, D_MODEL), lambda i: (i, 0)),
        out_shape=jax.ShapeDtypeStruct((rows, D_MODEL), F32),
        scratch_shapes=[pltpu.VMEM((tm + 2 * SUBLANES, CONV_WIDTH), F32)],
        compiler_params=_params(1),
        name="out_proj",
    )(ya, cb, ucv, ucv, ucv, h, conv_w, conv_b, conv_gain, w_out, seg)


def _final_body(h_ref, g_ref, o_ref):
    o_ref[0] = _rms(h_ref[...]) * g_ref[...]


def _final_norm(h, gain, batch, n, tm):
    lp = SEQ_PAD + n
    tiles = n // tm
    return pl.pallas_call(
        _final_body,
        grid=(batch, tiles),
        in_specs=[
            pl.BlockSpec((tm, D_MODEL), lambda b, i: (b * (lp // tm) + SEQ_PAD // tm + i, 0)),
            _const_spec((1, D_MODEL)),
        ],
        out_specs=pl.BlockSpec((1, tm, D_MODEL), lambda b, i: (b, i, 0)),
        out_shape=jax.ShapeDtypeStruct((batch, n, D_MODEL), F32),
        compiler_params=_params(2),
        name="final_norm",
    )(h, gain)


def _largest_tile(extent, cap):
    units = extent // LANES
    best = max(d for d in range(1, units + 1) if units % d == 0 and d * LANES <= cap)
    return best * LANES


def _in_column_order():
    halves_heads_pairs = [(half, r, i) for half in range(2) for r in range(KV_GROUP) for i in range(PAIRS)]
    cols = []
    for g in range(N_KV_HEADS):
        cols += [(KV_GROUP * g + r) * HEAD_DIM + 2 * i + half for half, r, i in halves_heads_pairs]
    for g in range(N_KV_HEADS):
        cols += [ATTN_WIDTH + g * HEAD_DIM + 2 * i + half for half, r, i in halves_heads_pairs]
    cols += list(range(ATTN_WIDTH + KV_WIDTH, IN_WIDTH))
    return np.asarray(cols, np.int32)


def _head_gain_order():
    return np.asarray([2 * i + half for half in range(2) for r in range(KV_GROUP) for i in range(PAIRS)], np.int32)


def _rope_tables(n):
    rows = n // GRID_W
    row = jnp.repeat(jnp.arange(rows, dtype=F32), GRID_W)
    col = jnp.tile(jnp.arange(GRID_W, dtype=F32), rows)
    row = jnp.concatenate([jnp.zeros((SEQ_PAD,), F32), row])
    col = jnp.concatenate([jnp.zeros((SEQ_PAD,), F32), col])
    freqs = ROPE_THETA ** (-jnp.arange(ROPE_PAIRS_AXIS, dtype=F32) / ROPE_PAIRS_AXIS)
    ang = jnp.concatenate([row[:, None] * freqs, col[:, None] * freqs], axis=-1)
    return jnp.tile(jnp.cos(ang), (1, KV_GROUP)), jnp.tile(jnp.sin(ang), (1, KV_GROUP))


def _block_diag_ones(size, block):
    idx = np.arange(size) // block
    return jnp.asarray(idx[:, None] == idx[None, :], BF16)


def _trunk(x, meta_tokens, p):
    batch, n, d = x.shape
    assert d == D_MODEL and n % (2 * KV_CHUNK) == 0 and n % GRID_W == 0
    lp = SEQ_PAD + n
    rows = batch * lp
    tm_ffn = _largest_tile(rows, 512)
    tm_seq = _largest_tile(lp, 768)
    tq = tm_seq
    h = jnp.concatenate([
        jnp.zeros((batch, SEQ_PAD - N_META, d), F32),
        jnp.broadcast_to(meta_tokens[None].astype(F32), (batch, N_META, d)),
        x.astype(F32)], axis=1).reshape(rows, d)
    cos, sin = _rope_tables(n)
    for l in range(p["depth"]):
        h = _ffn(h, p["ffn1_norm"][l], p["ffn1_w_gate"][l], p["ffn1_w_up"][l], p["ffn1_w_down"][l], tm_ffn)
        q, kt, v, cb, ucv = _in_proj(h, p["mix_norm"][l], p["w_in"][l], p["qk_gain"][l], cos, sin,
                                     p["seg_pairs"], lp, tm_seq)
        ya = _attention(q, kt, v, p["attn_gain"][l], batch, lp, tq)
        h = _out_proj(ya, cb, ucv, h, p["conv_w"][l], p["conv_b"][l], p["conv_gain"][l], p["w_out"][l],
                      p["seg_heads"], lp, tm_seq)
        h = _ffn(h, p["ffn2_norm"][l], p["ffn2_w_gate"][l], p["ffn2_w_up"][l], p["ffn2_w_down"][l], tm_ffn)
    return _final_norm(h, p["final_norm"], batch, n, LANES)


def kernel(x_prompt, x_sample, meta_tokens, ffn1_norm, ffn1_w_gate, ffn1_w_up, ffn1_w_down, mix_norm, w_in, q_norm, k_norm, conv_w, conv_b, attn_out_norm, conv_out_norm, w_out, ffn2_norm, ffn2_w_gate, ffn2_w_up, ffn2_w_down, final_norm):
    depth = w_in.shape[0]
    order = _head_gain_order()
    q_gain = q_norm[:, order] * (HEAD_DIM ** -0.5 * LOG2_E)
    k_gain = k_norm[:, order]
    p = {
        "depth": depth,
        "ffn1_norm": ffn1_norm[:, None, :],
        "ffn1_w_gate": ffn1_w_gate.astype(BF16),
        "ffn1_w_up": ffn1_w_up.astype(BF16),
        "ffn1_w_down": ffn1_w_down.astype(BF16),
        "mix_norm": mix_norm[:, None, :],
        "w_in": w_in[:, :, _in_column_order()].astype(BF16),
        "qk_gain": jnp.stack([q_gain, q_gain, k_gain, k_gain], axis=1),
        "attn_gain": jnp.pad(attn_out_norm.reshape(depth, N_KV_HEADS, KV_GROUP, HEAD_DIM),
                             ((0, 0), (0, 0), (0, 0), (0, LANES - HEAD_DIM))).reshape(depth, N_KV_HEADS, KV_GROUP, LANES),
        "conv_w": conv_w,
        "conv_b": conv_b[:, None, :],
        "conv_gain": conv_out_norm[:, None, :],
        "w_out": w_out.astype(BF16),
        "ffn2_norm": ffn2_norm[:, None, :],
        "ffn2_w_gate": ffn2_w_gate.astype(BF16),
        "ffn2_w_up": ffn2_w_up.astype(BF16),
        "ffn2_w_down": ffn2_w_down.astype(BF16),
        "final_norm": final_norm[None, :],
        "seg_pairs": _block_diag_ones(HALF_LANES, PAIRS),
        "seg_heads": _block_diag_ones(MXU_WIDTH, HEAD_DIM),
    }
    return (_trunk(x_prompt, meta_tokens, p), _trunk(x_sample, meta_tokens, p))
```

```python
import functools

import numpy as np
import jax
import jax.numpy as jnp
from jax import lax
from jax.experimental import pallas as pl
from jax.experimental.pallas import tpu as pltpu

F32 = jnp.float32
BF16 = jnp.bfloat16

D_MODEL = 1024
N_META = 16
GRID_W = 64
HEAD_DIM = 64
N_Q_HEADS = 8
N_KV_HEADS = 2
KV_GROUP = N_Q_HEADS // N_KV_HEADS
ATTN_WIDTH = N_Q_HEADS * HEAD_DIM
KV_WIDTH = N_KV_HEADS * HEAD_DIM
CONV_WIDTH = D_MODEL - ATTN_WIDTH
D_FF = 2816
ROPE_THETA = 10000.0
ROPE_PAIRS_AXIS = HEAD_DIM // 4
EPS = 1e-6
IN_WIDTH = ATTN_WIDTH + 2 * KV_WIDTH + 3 * CONV_WIDTH

LANES = 128
SUBLANES = 8
MXU_WIDTH = 256
SEQ_PAD = LANES
GROUP_LANES = KV_GROUP * HEAD_DIM
HALF_LANES = GROUP_LANES // 2
PAIRS = HEAD_DIM // 2
QK_COLS = 2 * N_KV_HEADS * GROUP_LANES
IN_COLS = QK_COLS + KV_WIDTH + 3 * CONV_WIDTH
KV_CHUNK = 4 * LANES
LOOP_CHUNKS = 4
FF_CHUNKS = ((0, 768), (768, 768), (1536, 768), (2304, 512))
MASKED_SCORE = -1e30
LOG2_E = 1.4426950408889634
VMEM_LIMIT = 56 * 1024 * 1024


def _const_spec(shape):
    zeros = (0,) * len(shape)
    return pl.BlockSpec(shape, lambda *_: zeros, pipeline_mode=pl.Buffered(1))


def _params(n_axes):
    return pltpu.CompilerParams(dimension_semantics=("arbitrary",) * n_axes, vmem_limit_bytes=VMEM_LIMIT)


def _rms(x):
    return x * lax.rsqrt(jnp.mean(x * x, axis=-1, keepdims=True) + EPS)


def _segment_sums(x, seg_ref):
    hi = x.astype(BF16)
    lo = (x - hi.astype(F32)).astype(BF16)
    return jnp.dot(jnp.concatenate([hi, lo], axis=1), seg_ref[...], preferred_element_type=F32)


def _ffn_body(h_ref, g_ref, wg_ref, wu_ref, wd_ref, o_ref, act_scr):
    x = h_ref[...]
    xn = (_rms(x) * g_ref[...]).astype(BF16)
    for start, size in FF_CHUNKS:
        gate = jnp.dot(xn, wg_ref[:, start:start + size], preferred_element_type=F32)
        up = jnp.dot(xn, wu_ref[:, start:start + size], preferred_element_type=F32)
        silu = gate * (1.0 / (1.0 + jnp.exp(-gate)))
        act_scr[:, start:start + size] = (silu * up).astype(BF16)
    o_ref[...] = x + 0.5 * jnp.dot(act_scr[...], wd_ref[...], preferred_element_type=F32)


def _ffn(h, gain, w_gate, w_up, w_down, tm):
    rows = h.shape[0]
    return pl.pallas_call(
        _ffn_body,
        grid=(rows // tm,),
        in_specs=[
            pl.BlockSpec((tm, D_MODEL), lambda i: (i, 0)),
            _const_spec((1, D_MODEL)),
            _const_spec((D_MODEL, D_FF)),
            _const_spec((D_MODEL, D_FF)),
            _const_spec((D_FF, D_MODEL)),
        ],
        out_specs=pl.BlockSpec((tm, D_MODEL), lambda i: (i, 0)),
        out_shape=jax.ShapeDtypeStruct((rows, D_MODEL), F32),
        scratch_shapes=[pltpu.VMEM((tm, D_FF), BF16)],
        compiler_params=_params(1),
        name="ffn",
    )(h, gain, w_gate, w_up, w_down)


def _in_body(h_ref, g_ref, w_ref, qkg_ref, cos_ref, sin_ref, seg_ref,
             q_ref, kt_ref, v_ref, cb_ref, ucv_ref, *, tm):
    xn = (_rms(h_ref[...]) * g_ref[...]).astype(BF16)
    u_all = jnp.dot(xn, w_ref[...], preferred_element_type=F32)
    cos = cos_ref[...]
    sin = sin_ref[...]
    for c in range(2 * N_KV_HEADS):
        x0 = u_all[:, c * GROUP_LANES:c * GROUP_LANES + HALF_LANES]
        x1 = u_all[:, c * GROUP_LANES + HALF_LANES:(c + 1) * GROUP_LANES]
        ssq = _segment_sums(x0 * x0 + x1 * x1, seg_ref)
        inv = lax.rsqrt(ssq * (1.0 / HEAD_DIM) + EPS)
        a0 = x0 * inv * qkg_ref[c:c + 1, :HALF_LANES]
        a1 = x1 * inv * qkg_ref[c:c + 1, HALF_LANES:]
        o0 = a0 * cos - a1 * sin
        o1 = a0 * sin + a1 * cos
        if c < N_KV_HEADS:
            q_ref[c, :, :HALF_LANES] = o0.astype(BF16)
            q_ref[c, :, HALF_LANES:] = o1.astype(BF16)
        else:
            for j in range(tm // LANES):
                rows = slice(j * LANES, (j + 1) * LANES)
                kt_ref[c - N_KV_HEADS, j, :HALF_LANES, :] = o0[rows, :].T.astype(BF16)
                kt_ref[c - N_KV_HEADS, j, HALF_LANES:, :] = o1[rows, :].T.astype(BF16)
    v = u_all[:, QK_COLS:QK_COLS + KV_WIDTH]
    lane = lax.broadcasted_iota(jnp.int32, v.shape, 1)
    v_ref[0] = jnp.where(lane < HEAD_DIM, v, 1.0).astype(BF16)
    v_ref[1] = jnp.where(lane < HEAD_DIM, pltpu.roll(v, HEAD_DIM, 1), 1.0).astype(BF16)
    conv0 = QK_COLS + KV_WIDTH
    cb_ref[...] = u_all[:, conv0:conv0 + CONV_WIDTH]
    ucv_ref[...] = (u_all[:, conv0 + CONV_WIDTH:conv0 + 2 * CONV_WIDTH]
                    * u_all[:, conv0 + 2 * CONV_WIDTH:conv0 + 3 * CONV_WIDTH])


def _in_proj(h, gain, w_in, qk_gain, cos, sin, seg, lp, tm):
    rows = h.shape[0]
    tiles_per_seq = lp // tm
    return pl.pallas_call(
        functools.partial(_in_body, tm=tm),
        grid=(rows // tm,),
        in_specs=[
            pl.BlockSpec((tm, D_MODEL), lambda i: (i, 0)),
            _const_spec((1, D_MODEL)),
            _const_spec((D_MODEL, IN_COLS)),
            _const_spec((2 * N_KV_HEADS, GROUP_LANES)),
            pl.BlockSpec((tm, HALF_LANES), lambda i: (i % tiles_per_seq, 0)),
            pl.BlockSpec((tm, HALF_LANES), lambda i: (i % tiles_per_seq, 0)),
            _const_spec((2 * HALF_LANES, HALF_LANES)),
        ],
        out_specs=[
            pl.BlockSpec((N_KV_HEADS, tm, GROUP_LANES), lambda i: (0, i, 0)),
            pl.BlockSpec((N_KV_HEADS, tm // LANES, GROUP_LANES, LANES), lambda i: (0, i, 0, 0)),
            pl.BlockSpec((N_KV_HEADS, tm, LANES), lambda i: (0, i, 0)),
            pl.BlockSpec((tm, CONV_WIDTH), lambda i: (i, 0)),
            pl.BlockSpec((tm, CONV_WIDTH), lambda i: (i, 0)),
        ],
        out_shape=[
            jax.ShapeDtypeStruct((N_KV_HEADS, rows, GROUP_LANES), BF16),
            jax.ShapeDtypeStruct((N_KV_HEADS, rows // LANES, GROUP_LANES, LANES), BF16),
            jax.ShapeDtypeStruct((N_KV_HEADS, rows, LANES), BF16),
            jax.ShapeDtypeStruct((rows, CONV_WIDTH), F32),
            jax.ShapeDtypeStruct((rows, CONV_WIDTH), F32),
        ],
        compiler_params=_params(1),
        name="in_proj",
    )(h, gain, w_in, qk_gain, cos, sin, seg)


def _attn_body(q_ref, kt_ref, v_ref, gain_ref, o_ref, qs_scr, s_scr, mx_scr, m_scr, acc_scr, *, tq, n_steps):
    q = q_ref[0].astype(F32)
    lane = lax.broadcasted_iota(jnp.int32, q.shape, 1)
    head_of_lane = (lane // PAIRS) % KV_GROUP
    for r in range(KV_GROUP):
        qs_scr[r * tq:(r + 1) * tq, :] = jnp.where(head_of_lane == r, q, 0.0).astype(BF16)

    def lane_tiles(x):
        return [x[:, j * LANES:(j + 1) * LANES] for j in range(x.shape[1] // LANES)]

    def produce(c, slot):
        kt4 = kt_ref[0, pl.ds(1 + c * (KV_CHUNK // LANES), KV_CHUNK // LANES)]
        kt = jnp.concatenate([kt4[j] for j in range(KV_CHUNK // LANES)], axis=1)
        s = jnp.dot(qs_scr[...], kt, preferred_element_type=F32)
        s_scr[slot] = s
        mx_scr[slot] = functools.reduce(jnp.maximum, lane_tiles(s))

    def accumulate(s_tiles, m_cur, v):
        m_old = m_scr[...]
        m_new = jnp.maximum(m_old, m_cur)
        p = jnp.concatenate([jnp.exp2(t - m_new) for t in s_tiles], axis=1).astype(BF16)
        pv = jnp.dot(p, v, preferred_element_type=F32)
        acc_scr[...] = jnp.exp2(m_old - m_new) * acc_scr[...] + pv
        m_scr[...] = m_new

    def consume(c, slot):
        v = v_ref[0, pl.ds(pl.multiple_of(SEQ_PAD + c * KV_CHUNK, LANES), KV_CHUNK), :]
        accumulate(lane_tiles(s_scr[slot]), jnp.max(mx_scr[slot], axis=1, keepdims=True), v)

    m_scr[...] = jnp.full(m_scr.shape, MASKED_SCORE, F32)
    acc_scr[...] = jnp.zeros(acc_scr.shape, F32)

    produce(0, 0)

    def trip(i, carry):
        for u in range(LOOP_CHUNKS):
            produce(i * LOOP_CHUNKS + u + 1, (u + 1) % 2)
            consume(i * LOOP_CHUNKS + u, u % 2)
        return carry

    looped = (n_steps - 2) // LOOP_CHUNKS * LOOP_CHUNKS
    lax.fori_loop(0, looped // LOOP_CHUNKS, trip, 0)
    for c in range(looped, n_steps - 1):
        produce(c + 1, (c + 1) % 2)
        consume(c, c % 2)
    s = jnp.dot(qs_scr[...], kt_ref[0, 0], preferred_element_type=F32)
    key = lax.broadcasted_iota(jnp.int32, s.shape, 1)
    s = jnp.where(key >= SEQ_PAD - N_META, s, MASKED_SCORE)
    consume(n_steps - 1, 1)
    accumulate([s], jnp.max(s, axis=1, keepdims=True), v_ref[0, 0:SEQ_PAD, :])

    acc = acc_scr[...]
    is_value = lax.broadcasted_iota(jnp.int32, (1, LANES), 1) < HEAD_DIM
    weight = jnp.where(is_value, 1.0 / HEAD_DIM, EPS / (LANES - HEAD_DIM))
    y = acc * lax.rsqrt(jnp.sum(acc * acc * weight, axis=1, keepdims=True))
    heads = [y[r * tq:(r + 1) * tq, :] * gain_ref[0, r:r + 1, :] for r in range(KV_GROUP)]
    is_value = lax.broadcasted_iota(jnp.int32, (tq, LANES), 1) < HEAD_DIM
    o_ref[0, :, :LANES] = jnp.where(is_value, heads[0], pltpu.roll(heads[1], HEAD_DIM, 1)).astype(BF16)
    o_ref[0, :, LANES:] = jnp.where(is_value, heads[2], pltpu.roll(heads[3], HEAD_DIM, 1)).astype(BF16)


def _attention(q, kt, v, gain, batch, lp, tq):
    rows = batch * lp
    q_tiles = lp // tq
    chunks = lp // LANES
    n_steps = (lp - SEQ_PAD) // KV_CHUNK
    return pl.pallas_call(
        functools.partial(_attn_body, tq=tq, n_steps=n_steps),
        grid=(batch, N_KV_HEADS, q_tiles),
        in_specs=[
            pl.BlockSpec((1, tq, GROUP_LANES), lambda b, g, i: (g, b * q_tiles + i, 0)),
            pl.BlockSpec((1, chunks, GROUP_LANES, LANES), lambda b, g, i: (g, b, 0, 0)),
            pl.BlockSpec((1, lp, LANES), lambda b, g, i: (g, b, 0)),
            pl.BlockSpec((1, KV_GROUP, LANES), lambda b, g, i: (g, 0, 0)),
        ],
        out_specs=pl.BlockSpec((1, tq, GROUP_LANES), lambda b, g, i: (g, b * q_tiles + i, 0)),
        out_shape=jax.ShapeDtypeStruct((N_KV_HEADS, rows, GROUP_LANES), BF16),
        scratch_shapes=[
            pltpu.VMEM((KV_GROUP * tq, GROUP_LANES), BF16),
            pltpu.VMEM((2, KV_GROUP * tq, KV_CHUNK), F32),
            pltpu.VMEM((2, KV_GROUP * tq, LANES), F32),
            pltpu.VMEM((KV_GROUP * tq, LANES), F32),
            pltpu.VMEM((KV_GROUP * tq, LANES), F32),
        ],
        compiler_params=_params(3),
        name="attn",
    )(q, kt, v, gain)


def _out_body(ya_ref, cb_ref, ucv_ref, prev_ref, next_ref, h_ref, cw_ref, cbias_ref, cgain_ref,
              wo_ref, seg_ref, o_ref, ext_scr, mix_scr, *, tm, lp):
    ext_scr[0:SUBLANES, :] = prev_ref[...]
    ext_scr[SUBLANES:SUBLANES + tm, :] = ucv_ref[...]
    ext_scr[SUBLANES + tm:, :] = next_ref[...]
    pos = (pl.program_id(0) % (lp // tm)) * tm + lax.broadcasted_iota(jnp.int32, (tm, 1), 0)
    left = jnp.where(pos == SEQ_PAD - N_META, 0.0, ext_scr[SUBLANES - 1:SUBLANES - 1 + tm, :])
    right = jnp.where(pos == lp - 1, 0.0, ext_scr[SUBLANES + 1:SUBLANES + 1 + tm, :])
    conv = left * cw_ref[0:1, :] + ucv_ref[...] * cw_ref[1:2, :] + right * cw_ref[2:3, :] + cbias_ref[...]
    y = cb_ref[...] * conv
    for g in range(N_KV_HEADS):
        mix_scr[:, g * GROUP_LANES:(g + 1) * GROUP_LANES] = ya_ref[g]
    for half in range(CONV_WIDTH // MXU_WIDTH):
        cols = slice(half * MXU_WIDTH, (half + 1) * MXU_WIDTH)
        yh = y[:, cols]
        inv = lax.rsqrt(_segment_sums(yh * yh, seg_ref) * (1.0 / HEAD_DIM) + EPS)
        mix_scr[:, ATTN_WIDTH + half * MXU_WIDTH:ATTN_WIDTH + (half + 1) * MXU_WIDTH] = (
            yh * inv * cgain_ref[:, cols]).astype(BF16)
    o_ref[...] = h_ref[...] + jnp.dot(mix_scr[...], wo_ref[...], preferred_element_type=F32)


def _out_proj(ya, cb, ucv, h, conv_w, conv_b, conv_gain, w_out, seg, lp, tm):
    rows = h.shape[0]
    halo_per_tile = tm // SUBLANES
    last_halo = rows // SUBLANES - 1
    return pl.pallas_call(
        functools.partial(_out_body, tm=tm, lp=lp),
        grid=(rows // tm,),
        in_specs=[
            pl.BlockSpec((N_KV_HEADS, tm, GROUP_LANES), lambda i: (0, i, 0)),
            pl.BlockSpec((tm, CONV_WIDTH), lambda i: (i, 0)),
            pl.BlockSpec((tm, CONV_WIDTH), lambda i: (i, 0)),
            pl.BlockSpec((SUBLANES, CONV_WIDTH), lambda i: (jnp.maximum(i * halo_per_tile - 1, 0), 0)),
            pl.BlockSpec((SUBLANES, CONV_WIDTH), lambda i: (jnp.minimum((i + 1) * halo_per_tile, last_halo), 0)),
            pl.BlockSpec((tm, D_MODEL), lambda i: (i, 0)),
            _const_spec((3, CONV_WIDTH)),
            _const_spec((1, CONV_WIDTH)),
            _const_spec((1, CONV_WIDTH)),
            _const_spec((D_MODEL, D_MODEL)),
            _const_spec((2 * MXU_WIDTH, MXU_WIDTH)),
        ],
        out_specs=pl.BlockSpec((tm, D_MODEL), lambda i: (i, 0)),
        out_shape=jax.ShapeDtypeStruct((rows, D_MODEL), F32),
        scratch_shapes=[pltpu.VMEM((tm + 2 * SUBLANES, CONV_WIDTH), F32), pltpu.VMEM((tm, D_MODEL), BF16)],
        compiler_params=_params(1),
        name="out_proj",
    )(ya, cb, ucv, ucv, ucv, h, conv_w, conv_b, conv_gain, w_out, seg)


def _final_body(h_ref, g_ref, o_ref):
    o_ref[0] = _rms(h_ref[...]) * g_ref[...]


def _final_norm(h, gain, batch, n, tm):
    lp = SEQ_PAD + n
    tiles = n // tm
    return pl.pallas_call(
        _final_body,
        grid=(batch, tiles),
        in_specs=[
            pl.BlockSpec((pl.Element(tm), pl.Element(D_MODEL)), lambda b, i: (pl.multiple_of(b * lp + SEQ_PAD + i * tm, LANES), 0)),
            _const_spec((1, D_MODEL)),
        ],
        out_specs=pl.BlockSpec((1, tm, D_MODEL), lambda b, i: (b, i, 0)),
        out_shape=jax.ShapeDtypeStruct((batch, n, D_MODEL), F32),
        compiler_params=_params(2),
        name="final_norm",
    )(h, gain)


def _largest_tile(extent, cap):
    units = extent // LANES
    best = max(d for d in range(1, units + 1) if units % d == 0 and d * LANES <= cap)
    return best * LANES


def _in_column_order():
    halves_heads_pairs = [(half, r, i) for half in range(2) for r in range(KV_GROUP) for i in range(PAIRS)]
    cols = []
    for g in range(N_KV_HEADS):
        cols += [(KV_GROUP * g + r) * HEAD_DIM + 2 * i + half for half, r, i in halves_heads_pairs]
    for g in range(N_KV_HEADS):
        cols += [ATTN_WIDTH + g * HEAD_DIM + 2 * i + half for half, r, i in halves_heads_pairs]
    cols += list(range(ATTN_WIDTH + KV_WIDTH, IN_WIDTH))
    return np.asarray(cols, np.int32)


def _head_gain_order():
    return np.asarray([2 * i + half for half in range(2) for r in range(KV_GROUP) for i in range(PAIRS)], np.int32)


def _rope_tables(n):
    rows = n // GRID_W
    row = jnp.repeat(jnp.arange(rows, dtype=F32), GRID_W)
    col = jnp.tile(jnp.arange(GRID_W, dtype=F32), rows)
    row = jnp.concatenate([jnp.zeros((SEQ_PAD,), F32), row])
    col = jnp.concatenate([jnp.zeros((SEQ_PAD,), F32), col])
    freqs = ROPE_THETA ** (-jnp.arange(ROPE_PAIRS_AXIS, dtype=F32) / ROPE_PAIRS_AXIS)
    ang = jnp.concatenate([row[:, None] * freqs, col[:, None] * freqs], axis=-1)
    return jnp.tile(jnp.cos(ang), (1, KV_GROUP)), jnp.tile(jnp.sin(ang), (1, KV_GROUP))


def _stacked_block_diag_ones(size, block):
    idx = np.arange(size) // block
    seg = idx[:, None] == idx[None, :]
    return jnp.asarray(np.concatenate([seg, seg], axis=0), BF16)


def _trunk(x, meta_tokens, p):
    batch, n, d = x.shape
    assert d == D_MODEL and n % (2 * KV_CHUNK) == 0 and n % GRID_W == 0
    lp = SEQ_PAD + n
    rows = batch * lp
    tm_ffn = _largest_tile(rows, 512)
    tm_seq = _largest_tile(lp, 768)
    tq = tm_seq
    h = jnp.concatenate([
        jnp.zeros((batch, SEQ_PAD - N_META, d), F32),
        jnp.broadcast_to(meta_tokens[None].astype(F32), (batch, N_META, d)),
        x.astype(F32)], axis=1).reshape(rows, d)
    cos, sin = _rope_tables(n)
    for l in range(p["depth"]):
        h = _ffn(h, p["ffn1_norm"][l], p["ffn1_w_gate"][l], p["ffn1_w_up"][l], p["ffn1_w_down"][l], tm_ffn)
        q, kt, v, cb, ucv = _in_proj(h, p["mix_norm"][l], p["w_in"][l], p["qk_gain"][l], cos, sin,
                                     p["seg_pairs"], lp, tm_seq)
        ya = _attention(q, kt, v, p["attn_gain"][l], batch, lp, tq)
        h = _out_proj(ya, cb, ucv, h, p["conv_w"][l], p["conv_b"][l], p["conv_gain"][l], p["w_out"][l],
                      p["seg_heads"], lp, tm_seq)
        h = _ffn(h, p["ffn2_norm"][l], p["ffn2_w_gate"][l], p["ffn2_w_up"][l], p["ffn2_w_down"][l], tm_ffn)
    return _final_norm(h, p["final_norm"], batch, n, 2 * KV_CHUNK)


def kernel(x_prompt, x_sample, meta_tokens, ffn1_norm, ffn1_w_gate, ffn1_w_up, ffn1_w_down, mix_norm, w_in, q_norm, k_norm, conv_w, conv_b, attn_out_norm, conv_out_norm, w_out, ffn2_norm, ffn2_w_gate, ffn2_w_up, ffn2_w_down, final_norm):
    depth = w_in.shape[0]
    order = _head_gain_order()
    q_gain = q_norm[:, order] * (HEAD_DIM ** -0.5 * LOG2_E)
    k_gain = k_norm[:, order]
    p = {
        "depth": depth,
        "ffn1_norm": ffn1_norm[:, None, :],
        "ffn1_w_gate": ffn1_w_gate.astype(BF16),
        "ffn1_w_up": ffn1_w_up.astype(BF16),
        "ffn1_w_down": ffn1_w_down.astype(BF16),
        "mix_norm": mix_norm[:, None, :],
        "w_in": w_in[:, :, _in_column_order()].astype(BF16),
        "qk_gain": jnp.stack([q_gain, q_gain, k_gain, k_gain], axis=1),
        "attn_gain": jnp.pad(attn_out_norm.reshape(depth, N_KV_HEADS, KV_GROUP, HEAD_DIM),
                             ((0, 0), (0, 0), (0, 0), (0, LANES - HEAD_DIM))).reshape(depth, N_KV_HEADS, KV_GROUP, LANES),
        "conv_w": conv_w,
        "conv_b": conv_b[:, None, :],
        "conv_gain": conv_out_norm[:, None, :],
        "w_out": w_out.astype(BF16),
        "ffn2_norm": ffn2_norm[:, None, :],
        "ffn2_w_gate": ffn2_w_gate.astype(BF16),
        "ffn2_w_up": ffn2_w_up.astype(BF16),
        "ffn2_w_down": ffn2_w_down.astype(BF16),
        "final_norm": final_norm[None, :],
        "seg_pairs": _stacked_block_diag_ones(HALF_LANES, PAIRS),
        "seg_heads": _stacked_block_diag_ones(MXU_WIDTH, HEAD_DIM),
    }
    return (_trunk(x_prompt, meta_tokens, p), _trunk(x_sample, meta_tokens, p))
```

```python
import functools

import numpy as np
import jax
import jax.numpy as jnp
from jax import lax
from jax.experimental import pallas as pl
from jax.experimental.pallas import tpu as pltpu

F32 = jnp.float32
BF16 = jnp.bfloat16

D_MODEL = 1024
N_META = 16
GRID_W = 64
HEAD_DIM = 64
N_Q_HEADS = 8
N_KV_HEADS = 2
KV_GROUP = N_Q_HEADS // N_KV_HEADS
ATTN_WIDTH = N_Q_HEADS * HEAD_DIM
KV_WIDTH = N_KV_HEADS * HEAD_DIM
CONV_WIDTH = D_MODEL - ATTN_WIDTH
D_FF = 2816
ROPE_THETA = 10000.0
ROPE_PAIRS_AXIS = HEAD_DIM // 4
EPS = 1e-6
IN_WIDTH = ATTN_WIDTH + 2 * KV_WIDTH + 3 * CONV_WIDTH

LANES = 128
SUBLANES = 8
MXU_WIDTH = 256
SEQ_PAD = LANES
GROUP_LANES = KV_GROUP * HEAD_DIM
HALF_LANES = GROUP_LANES // 2
PAIRS = HEAD_DIM // 2
QK_COLS = 2 * N_KV_HEADS * GROUP_LANES
IN_COLS = QK_COLS + KV_WIDTH + 3 * CONV_WIDTH
KV_CHUNK = 4 * LANES
LOOP_CHUNKS = 4
FF_CHUNKS = ((0, 768), (768, 768), (1536, 768), (2304, 512))
MASKED_SCORE = -1e30
LOG2_E = 1.4426950408889634
VMEM_LIMIT = 56 * 1024 * 1024


def _const_spec(shape):
    zeros = (0,) * len(shape)
    return pl.BlockSpec(shape, lambda *_: zeros, pipeline_mode=pl.Buffered(1))


def _params(n_axes):
    return pltpu.CompilerParams(dimension_semantics=("arbitrary",) * n_axes, vmem_limit_bytes=VMEM_LIMIT)


def _rms(x):
    return x * lax.rsqrt(jnp.mean(x * x, axis=-1, keepdims=True) + EPS)


def _segment_sums(x, seg_ref):
    hi = x.astype(BF16)
    lo = (x - hi.astype(F32)).astype(BF16)
    return jnp.dot(jnp.concatenate([hi, lo], axis=1), seg_ref[...], preferred_element_type=F32)


def _ffn_body(h_ref, g_ref, wg_ref, wu_ref, wd_ref, o_ref, act_scr):
    x = h_ref[...]
    xn = (_rms(x) * g_ref[...]).astype(BF16)
    for start, size in FF_CHUNKS:
        gate = jnp.dot(xn, wg_ref[:, start:start + size], preferred_element_type=F32)
        up = jnp.dot(xn, wu_ref[:, start:start + size], preferred_element_type=F32)
        silu = gate * (1.0 / (1.0 + jnp.exp(-gate)))
        act_scr[:, start:start + size] = (silu * up).astype(BF16)
    o_ref[...] = x + 0.5 * jnp.dot(act_scr[...], wd_ref[...], preferred_element_type=F32)


def _ffn(h, gain, w_gate, w_up, w_down, tm):
    rows = h.shape[0]
    return pl.pallas_call(
        _ffn_body,
        grid=(rows // tm,),
        in_specs=[
            pl.BlockSpec((tm, D_MODEL), lambda i: (i, 0)),
            _const_spec((1, D_MODEL)),
            _const_spec((D_MODEL, D_FF)),
            _const_spec((D_MODEL, D_FF)),
            _const_spec((D_FF, D_MODEL)),
        ],
        out_specs=pl.BlockSpec((tm, D_MODEL), lambda i: (i, 0)),
        out_shape=jax.ShapeDtypeStruct((rows, D_MODEL), F32),
        scratch_shapes=[pltpu.VMEM((tm, D_FF), BF16)],
        compiler_params=_params(1),
        name="ffn",
    )(h, gain, w_gate, w_up, w_down)


def _in_body(h_ref, g_ref, w_ref, qkg_ref, cos_ref, sin_ref, seg_ref,
             q_ref, k_ref, vt_ref, cb_ref, ucv_ref, *, tm):
    xn = (_rms(h_ref[...]) * g_ref[...]).astype(BF16)
    u_all = jnp.dot(xn, w_ref[...], preferred_element_type=F32)
    cos = cos_ref[...]
    sin = sin_ref[...]
    for c in range(2 * N_KV_HEADS):
        x0 = u_all[:, c * GROUP_LANES:c * GROUP_LANES + HALF_LANES]
        x1 = u_all[:, c * GROUP_LANES + HALF_LANES:(c + 1) * GROUP_LANES]
        ssq = _segment_sums(x0 * x0 + x1 * x1, seg_ref)
        inv = lax.rsqrt(ssq * (1.0 / HEAD_DIM) + EPS)
        a0 = x0 * inv * qkg_ref[c:c + 1, :HALF_LANES]
        a1 = x1 * inv * qkg_ref[c:c + 1, HALF_LANES:]
        o0 = a0 * cos - a1 * sin
        o1 = a0 * sin + a1 * cos
        out_ref = q_ref if c < N_KV_HEADS else k_ref
        out_ref[c % N_KV_HEADS, :, :HALF_LANES] = o0.astype(BF16)
        out_ref[c % N_KV_HEADS, :, HALF_LANES:] = o1.astype(BF16)
    v = u_all[:, QK_COLS:QK_COLS + KV_WIDTH]
    lane = lax.broadcasted_iota(jnp.int32, v.shape, 1)
    for g in range(N_KV_HEADS):
        v_ones = jnp.where(lane < HEAD_DIM, v if g == 0 else pltpu.roll(v, HEAD_DIM, 1), 1.0)
        for j in range(tm // LANES):
            vt_ref[g, j] = v_ones[j * LANES:(j + 1) * LANES, :].T.astype(BF16)
    conv0 = QK_COLS + KV_WIDTH
    cb_ref[...] = u_all[:, conv0:conv0 + CONV_WIDTH]
    ucv_ref[...] = (u_all[:, conv0 + CONV_WIDTH:conv0 + 2 * CONV_WIDTH]
                    * u_all[:, conv0 + 2 * CONV_WIDTH:conv0 + 3 * CONV_WIDTH])


def _in_proj(h, gain, w_in, qk_gain, cos, sin, seg, lp, tm):
    rows = h.shape[0]
    tiles_per_seq = lp // tm
    return pl.pallas_call(
        functools.partial(_in_body, tm=tm),
        grid=(rows // tm,),
        in_specs=[
            pl.BlockSpec((tm, D_MODEL), lambda i: (i, 0)),
            _const_spec((1, D_MODEL)),
            _const_spec((D_MODEL, IN_COLS)),
            _const_spec((2 * N_KV_HEADS, GROUP_LANES)),
            pl.BlockSpec((tm, HALF_LANES), lambda i: (i % tiles_per_seq, 0)),
            pl.BlockSpec((tm, HALF_LANES), lambda i: (i % tiles_per_seq, 0)),
            _const_spec((2 * HALF_LANES, HALF_LANES)),
        ],
        out_specs=[
            pl.BlockSpec((N_KV_HEADS, tm, GROUP_LANES), lambda i: (0, i, 0)),
            pl.BlockSpec((N_KV_HEADS, tm, GROUP_LANES), lambda i: (0, i, 0)),
            pl.BlockSpec((N_KV_HEADS, tm // LANES, LANES, LANES), lambda i: (0, i, 0, 0)),
            pl.BlockSpec((tm, CONV_WIDTH), lambda i: (i, 0)),
            pl.BlockSpec((tm, CONV_WIDTH), lambda i: (i, 0)),
        ],
        out_shape=[
            jax.ShapeDtypeStruct((N_KV_HEADS, rows, GROUP_LANES), BF16),
            jax.ShapeDtypeStruct((N_KV_HEADS, rows, GROUP_LANES), BF16),
            jax.ShapeDtypeStruct((N_KV_HEADS, rows // LANES, LANES, LANES), BF16),
            jax.ShapeDtypeStruct((rows, CONV_WIDTH), F32),
            jax.ShapeDtypeStruct((rows, CONV_WIDTH), F32),
        ],
        compiler_params=_params(1),
        name="in_proj",
    )(h, gain, w_in, qk_gain, cos, sin, seg)


def _attn_body(q_ref, k_ref, vt_ref, gain_ref, o_ref, qt_scr, s_scr, mx_scr, m_scr, acc_scr, *, tq, n_steps):
    qt = q_ref[0].astype(F32).T
    dim = lax.broadcasted_iota(jnp.int32, qt.shape, 0)
    head_of_dim = (dim // PAIRS) % KV_GROUP
    for r in range(KV_GROUP):
        qt_scr[:, r * tq:(r + 1) * tq] = jnp.where(head_of_dim == r, qt, 0.0).astype(BF16)

    def sublane_tiles(x):
        return [x[j * SUBLANES:(j + 1) * SUBLANES, :] for j in range(x.shape[0] // SUBLANES)]

    def produce(c, slot):
        k = k_ref[0, pl.ds(pl.multiple_of(SEQ_PAD + c * KV_CHUNK, LANES), KV_CHUNK), :]
        s = jnp.dot(k, qt_scr[...], preferred_element_type=F32)
        s_scr[slot] = s
        mx_scr[slot] = functools.reduce(jnp.maximum, sublane_tiles(s))

    def accumulate(s, m_cur, vt):
        m_old = m_scr[...]
        m_new = jnp.maximum(m_old, m_cur)
        p = jnp.exp2(s - m_new).astype(BF16)
        pv = jnp.dot(vt, p, preferred_element_type=F32)
        acc_scr[...] = jnp.exp2(m_old - m_new) * acc_scr[...] + pv
        m_scr[...] = m_new

    def consume(c, slot):
        vt4 = vt_ref[0, pl.ds(1 + c * (KV_CHUNK // LANES), KV_CHUNK // LANES)]
        vt = jnp.concatenate([vt4[j] for j in range(KV_CHUNK // LANES)], axis=1)
        accumulate(s_scr[slot], jnp.max(mx_scr[slot], axis=0, keepdims=True), vt)

    m_scr[...] = jnp.full(m_scr.shape, MASKED_SCORE, F32)
    acc_scr[...] = jnp.zeros(acc_scr.shape, F32)

    produce(0, 0)

    def trip(i, carry):
        for u in range(LOOP_CHUNKS):
            produce(i * LOOP_CHUNKS + u + 1, (u + 1) % 2)
            consume(i * LOOP_CHUNKS + u, u % 2)
        return carry

    looped = (n_steps - 2) // LOOP_CHUNKS * LOOP_CHUNKS
    lax.fori_loop(0, looped // LOOP_CHUNKS, trip, 0)
    for c in range(looped, n_steps - 1):
        produce(c + 1, (c + 1) % 2)
        consume(c, c % 2)
    s = jnp.dot(k_ref[0, 0:SEQ_PAD, :], qt_scr[...], preferred_element_type=F32)
    key = lax.broadcasted_iota(jnp.int32, s.shape, 0)
    s = jnp.where(key >= SEQ_PAD - N_META, s, MASKED_SCORE)
    consume(n_steps - 1, 1)
    accumulate(s, jnp.max(s, axis=0, keepdims=True), vt_ref[0, 0])

    acc = acc_scr[...]
    is_value = lax.broadcasted_iota(jnp.int32, acc.shape, 0) < HEAD_DIM
    weight = jnp.where(is_value, 1.0 / HEAD_DIM, EPS / (LANES - HEAD_DIM))
    yt = acc * lax.rsqrt(jnp.sum(acc * acc * weight, axis=0, keepdims=True))
    heads = [yt[:, r * tq:(r + 1) * tq].T * gain_ref[0, r:r + 1, :] for r in range(KV_GROUP)]
    is_value = lax.broadcasted_iota(jnp.int32, (tq, LANES), 1) < HEAD_DIM
    o_ref[0, :, :LANES] = jnp.where(is_value, heads[0], pltpu.roll(heads[1], HEAD_DIM, 1)).astype(BF16)
    o_ref[0, :, LANES:] = jnp.where(is_value, heads[2], pltpu.roll(heads[3], HEAD_DIM, 1)).astype(BF16)


def _attention(q, k, vt, gain, batch, lp, tq):
    rows = batch * lp
    q_tiles = lp // tq
    chunks = lp // LANES
    n_steps = (lp - SEQ_PAD) // KV_CHUNK
    return pl.pallas_call(
        functools.partial(_attn_body, tq=tq, n_steps=n_steps),
        grid=(batch, N_KV_HEADS, q_tiles),
        in_specs=[
            pl.BlockSpec((1, tq, GROUP_LANES), lambda b, g, i: (g, b * q_tiles + i, 0)),
            pl.BlockSpec((1, lp, GROUP_LANES), lambda b, g, i: (g, b, 0)),
            pl.BlockSpec((1, chunks, LANES, LANES), lambda b, g, i: (g, b, 0, 0)),
            pl.BlockSpec((1, KV_GROUP, LANES), lambda b, g, i: (g, 0, 0)),
        ],
        out_specs=pl.BlockSpec((1, tq, GROUP_LANES), lambda b, g, i: (g, b * q_tiles + i, 0)),
        out_shape=jax.ShapeDtypeStruct((N_KV_HEADS, rows, GROUP_LANES), BF16),
        scratch_shapes=[
            pltpu.VMEM((GROUP_LANES, KV_GROUP * tq), BF16),
            pltpu.VMEM((2, KV_CHUNK, KV_GROUP * tq), F32),
            pltpu.VMEM((2, SUBLANES, KV_GROUP * tq), F32),
            pltpu.VMEM((1, KV_GROUP * tq), F32),
            pltpu.VMEM((LANES, KV_GROUP * tq), F32),
        ],
        compiler_params=_params(3),
        name="attn",
    )(q, k, vt, gain)


def _out_body(ya_ref, cb_ref, ucv_ref, prev_ref, next_ref, h_ref, cw_ref, cbias_ref, cgain_ref,
              wo_ref, seg_ref, o_ref, ext_scr, mix_scr, *, tm, lp):
    ext_scr[0:SUBLANES, :] = prev_ref[...]
    ext_scr[SUBLANES:SUBLANES + tm, :] = ucv_ref[...]
    ext_scr[SUBLANES + tm:, :] = next_ref[...]
    pos = (pl.program_id(0) % (lp // tm)) * tm + lax.broadcasted_iota(jnp.int32, (tm, 1), 0)
    left = jnp.where(pos == SEQ_PAD - N_META, 0.0, ext_scr[SUBLANES - 1:SUBLANES - 1 + tm, :])
    right = jnp.where(pos == lp - 1, 0.0, ext_scr[SUBLANES + 1:SUBLANES + 1 + tm, :])
    conv = left * cw_ref[0:1, :] + ucv_ref[...] * cw_ref[1:2, :] + right * cw_ref[2:3, :] + cbias_ref[...]
    y = cb_ref[...] * conv
    for g in range(N_KV_HEADS):
        mix_scr[:, g * GROUP_LANES:(g + 1) * GROUP_LANES] = ya_ref[g]
    for half in range(CONV_WIDTH // MXU_WIDTH):
        cols = slice(half * MXU_WIDTH, (half + 1) * MXU_WIDTH)
        yh = y[:, cols]
        inv = lax.rsqrt(_segment_sums(yh * yh, seg_ref) * (1.0 / HEAD_DIM) + EPS)
        mix_scr[:, ATTN_WIDTH + half * MXU_WIDTH:ATTN_WIDTH + (half + 1) * MXU_WIDTH] = (
            yh * inv * cgain_ref[:, cols]).astype(BF16)
    o_ref[...] = h_ref[...] + jnp.dot(mix_scr[...], wo_ref[...], preferred_element_type=F32)


def _out_proj(ya, cb, ucv, h, conv_w, conv_b, conv_gain, w_out, seg, lp, tm):
    rows = h.shape[0]
    halo_per_tile = tm // SUBLANES
    last_halo = rows // SUBLANES - 1
    return pl.pallas_call(
        functools.partial(_out_body, tm=tm, lp=lp),
        grid=(rows // tm,),
        in_specs=[
            pl.BlockSpec((N_KV_HEADS, tm, GROUP_LANES), lambda i: (0, i, 0)),
            pl.BlockSpec((tm, CONV_WIDTH), lambda i: (i, 0)),
            pl.BlockSpec((tm, CONV_WIDTH), lambda i: (i, 0)),
            pl.BlockSpec((SUBLANES, CONV_WIDTH), lambda i: (jnp.maximum(i * halo_per_tile - 1, 0), 0)),
            pl.BlockSpec((SUBLANES, CONV_WIDTH), lambda i: (jnp.minimum((i + 1) * halo_per_tile, last_halo), 0)),
            pl.BlockSpec((tm, D_MODEL), lambda i: (i, 0)),
            _const_spec((3, CONV_WIDTH)),
            _const_spec((1, CONV_WIDTH)),
            _const_spec((1, CONV_WIDTH)),
            _const_spec((D_MODEL, D_MODEL)),
            _const_spec((2 * MXU_WIDTH, MXU_WIDTH)),
        ],
        out_specs=pl.BlockSpec((tm, D_MODEL), lambda i: (i, 0)),
        out_shape=jax.ShapeDtypeStruct((rows, D_MODEL), F32),
        scratch_shapes=[pltpu.VMEM((tm + 2 * SUBLANES, CONV_WIDTH), F32), pltpu.VMEM((tm, D_MODEL), BF16)],
        compiler_params=_params(1),
        name="out_proj",
    )(ya, cb, ucv, ucv, ucv, h, conv_w, conv_b, conv_gain, w_out, seg)


def _final_body(h_ref, g_ref, o_ref):
    o_ref[0] = _rms(h_ref[...]) * g_ref[...]


def _final_norm(h, gain, batch, n, tm):
    lp = SEQ_PAD + n
    tiles = n // tm
    return pl.pallas_call(
        _final_body,
        grid=(batch, tiles),
        in_specs=[
            pl.BlockSpec((pl.Element(tm), pl.Element(D_MODEL)), lambda b, i: (pl.multiple_of(b * lp + SEQ_PAD + i * tm, LANES), 0)),
            _const_spec((1, D_MODEL)),
        ],
        out_specs=pl.BlockSpec((1, tm, D_MODEL), lambda b, i: (b, i, 0)),
        out_shape=jax.ShapeDtypeStruct((batch, n, D_MODEL), F32),
        compiler_params=_params(2),
        name="final_norm",
    )(h, gain)


def _largest_tile(extent, cap):
    units = extent // LANES
    best = max(d for d in range(1, units + 1) if units % d == 0 and d * LANES <= cap)
    return best * LANES


def _in_column_order():
    halves_heads_pairs = [(half, r, i) for half in range(2) for r in range(KV_GROUP) for i in range(PAIRS)]
    cols = []
    for g in range(N_KV_HEADS):
        cols += [(KV_GROUP * g + r) * HEAD_DIM + 2 * i + half for half, r, i in halves_heads_pairs]
    for g in range(N_KV_HEADS):
        cols += [ATTN_WIDTH + g * HEAD_DIM + 2 * i + half for half, r, i in halves_heads_pairs]
    cols += list(range(ATTN_WIDTH + KV_WIDTH, IN_WIDTH))
    return np.asarray(cols, np.int32)


def _head_gain_order():
    return np.asarray([2 * i + half for half in range(2) for r in range(KV_GROUP) for i in range(PAIRS)], np.int32)


def _rope_tables(n):
    rows = n // GRID_W
    row = jnp.repeat(jnp.arange(rows, dtype=F32), GRID_W)
    col = jnp.tile(jnp.arange(GRID_W, dtype=F32), rows)
    row = jnp.concatenate([jnp.zeros((SEQ_PAD,), F32), row])
    col = jnp.concatenate([jnp.zeros((SEQ_PAD,), F32), col])
    freqs = ROPE_THETA ** (-jnp.arange(ROPE_PAIRS_AXIS, dtype=F32) / ROPE_PAIRS_AXIS)
    ang = jnp.concatenate([row[:, None] * freqs, col[:, None] * freqs], axis=-1)
    return jnp.tile(jnp.cos(ang), (1, KV_GROUP)), jnp.tile(jnp.sin(ang), (1, KV_GROUP))


def _stacked_block_diag_ones(size, block):
    idx = np.arange(size) // block
    seg = idx[:, None] == idx[None, :]
    return jnp.asarray(np.concatenate([seg, seg], axis=0), BF16)


def _trunk(x, meta_tokens, p):
    batch, n, d = x.shape
    assert d == D_MODEL and n % (2 * KV_CHUNK) == 0 and n % GRID_W == 0
    lp = SEQ_PAD + n
    rows = batch * lp
    tm_ffn = _largest_tile(rows, 512)
    tm_seq = _largest_tile(lp, 768)
    tq = tm_seq
    h = jnp.concatenate([
        jnp.zeros((batch, SEQ_PAD - N_META, d), F32),
        jnp.broadcast_to(meta_tokens[None].astype(F32), (batch, N_META, d)),
        x.astype(F32)], axis=1).reshape(rows, d)
    cos, sin = _rope_tables(n)
    for l in range(p["depth"]):
        h = _ffn(h, p["ffn1_norm"][l], p["ffn1_w_gate"][l], p["ffn1_w_up"][l], p["ffn1_w_down"][l], tm_ffn)
        q, k, vt, cb, ucv = _in_proj(h, p["mix_norm"][l], p["w_in"][l], p["qk_gain"][l], cos, sin,
                                     p["seg_pairs"], lp, tm_seq)
        ya = _attention(q, k, vt, p["attn_gain"][l], batch, lp, tq)
        h = _out_proj(ya, cb, ucv, h, p["conv_w"][l], p["conv_b"][l], p["conv_gain"][l], p["w_out"][l],
                      p["seg_heads"], lp, tm_seq)
        h = _ffn(h, p["ffn2_norm"][l], p["ffn2_w_gate"][l], p["ffn2_w_up"][l], p["ffn2_w_down"][l], tm_ffn)
    return _final_norm(h, p["final_norm"], batch, n, 2 * KV_CHUNK)


def kernel(x_prompt, x_sample, meta_tokens, ffn1_norm, ffn1_w_gate, ffn1_w_up, ffn1_w_down, mix_norm, w_in, q_norm, k_norm, conv_w, conv_b, attn_out_norm, conv_out_norm, w_out, ffn2_norm, ffn2_w_gate, ffn2_w_up, ffn2_w_down, final_norm):
    depth = w_in.shape[0]
    order = _head_gain_order()
    q_gain = q_norm[:, order] * (HEAD_DIM ** -0.5 * LOG2_E)
    k_gain = k_norm[:, order]
    p = {
        "depth": depth,
        "ffn1_norm": ffn1_norm[:, None, :],
        "ffn1_w_gate": ffn1_w_gate.astype(BF16),
        "ffn1_w_up": ffn1_w_up.astype(BF16),
        "ffn1_w_down": ffn1_w_down.astype(BF16),
        "mix_norm": mix_norm[:, None, :],
        "w_in": w_in[:, :, _in_column_order()].astype(BF16),
        "qk_gain": jnp.stack([q_gain, q_gain, k_gain, k_gain], axis=1),
        "attn_gain": jnp.pad(attn_out_norm.reshape(depth, N_KV_HEADS, KV_GROUP, HEAD_DIM),
                             ((0, 0), (0, 0), (0, 0), (0, LANES - HEAD_DIM))).reshape(depth, N_KV_HEADS, KV_GROUP, LANES),
        "conv_w": conv_w,
        "conv_b": conv_b[:, None, :],
        "conv_gain": conv_out_norm[:, None, :],
        "w_out": w_out.astype(BF16),
        "ffn2_norm": ffn2_norm[:, None, :],
        "ffn2_w_gate": ffn2_w_gate.astype(BF16),
        "ffn2_w_up": ffn2_w_up.astype(BF16),
        "ffn2_w_down": ffn2_w_down.astype(BF16),
        "final_norm": final_norm[None, :],
        "seg_pairs": _stacked_block_diag_ones(HALF_LANES, PAIRS),
        "seg_heads": _stacked_block_diag_ones(MXU_WIDTH, HEAD_DIM),
    }
    return (_trunk(x_prompt, meta_tokens, p), _trunk(x_sample, meta_tokens, p))
```

```python
import functools

import numpy as np
import jax
import jax.numpy as jnp
from jax import lax
from jax.experimental import pallas as pl
from jax.experimental.pallas import tpu as pltpu

F32 = jnp.float32
BF16 = jnp.bfloat16

D_MODEL = 1024
N_META = 16
GRID_W = 64
HEAD_DIM = 64
N_Q_HEADS = 8
N_KV_HEADS = 2
KV_GROUP = N_Q_HEADS // N_KV_HEADS
ATTN_WIDTH = N_Q_HEADS * HEAD_DIM
KV_WIDTH = N_KV_HEADS * HEAD_DIM
CONV_WIDTH = D_MODEL - ATTN_WIDTH
D_FF = 2816
ROPE_THETA = 10000.0
ROPE_PAIRS_AXIS = HEAD_DIM // 4
EPS = 1e-6
IN_WIDTH = ATTN_WIDTH + 2 * KV_WIDTH + 3 * CONV_WIDTH

LANES = 128
SUBLANES = 8
MXU_WIDTH = 256
SEQ_PAD = LANES
GROUP_LANES = KV_GROUP * HEAD_DIM
HALF_LANES = GROUP_LANES // 2
PAIRS = HEAD_DIM // 2
QK_COLS = 2 * N_KV_HEADS * GROUP_LANES
IN_COLS = QK_COLS + KV_WIDTH + 3 * CONV_WIDTH
ONES_ROWS = 16
VT_ROWS = HEAD_DIM + ONES_ROWS
KV_CHUNK = 4 * LANES
LOOP_CHUNKS = 4
FF_CHUNKS = ((0, 768), (768, 768), (1536, 768), (2304, 512))
MASKED_SCORE = -1e30
LOG2_E = 1.4426950408889634
VMEM_LIMIT = 56 * 1024 * 1024


def _const_spec(shape):
    zeros = (0,) * len(shape)
    return pl.BlockSpec(shape, lambda *_: zeros, pipeline_mode=pl.Buffered(1))


def _params(n_axes):
    return pltpu.CompilerParams(dimension_semantics=("arbitrary",) * n_axes, vmem_limit_bytes=VMEM_LIMIT)


def _rms(x):
    return x * lax.rsqrt(jnp.mean(x * x, axis=-1, keepdims=True) + EPS)


def _segment_sums(x, seg_ref):
    hi = x.astype(BF16)
    lo = (x - hi.astype(F32)).astype(BF16)
    return jnp.dot(jnp.concatenate([hi, lo], axis=1), seg_ref[...], preferred_element_type=F32)


def _ffn_body(h_ref, g_ref, wg_ref, wu_ref, wd_ref, o_ref, act_scr):
    x = h_ref[...]
    xn = (_rms(x) * g_ref[...]).astype(BF16)
    for start, size in FF_CHUNKS:
        gate = jnp.dot(xn, wg_ref[:, start:start + size], preferred_element_type=F32)
        up = jnp.dot(xn, wu_ref[:, start:start + size], preferred_element_type=F32)
        silu = gate * (1.0 / (1.0 + jnp.exp(-gate)))
        act_scr[:, start:start + size] = (silu * up).astype(BF16)
    o_ref[...] = x + 0.5 * jnp.dot(act_scr[...], wd_ref[...], preferred_element_type=F32)


def _ffn(h, gain, w_gate, w_up, w_down, tm):
    rows = h.shape[0]
    return pl.pallas_call(
        _ffn_body,
        grid=(rows // tm,),
        in_specs=[
            pl.BlockSpec((tm, D_MODEL), lambda i: (i, 0)),
            _const_spec((1, D_MODEL)),
            _const_spec((D_MODEL, D_FF)),
            _const_spec((D_MODEL, D_FF)),
            _const_spec((D_FF, D_MODEL)),
        ],
        out_specs=pl.BlockSpec((tm, D_MODEL), lambda i: (i, 0)),
        out_shape=jax.ShapeDtypeStruct((rows, D_MODEL), F32),
        scratch_shapes=[pltpu.VMEM((tm, D_FF), BF16)],
        compiler_params=_params(1),
        name="ffn",
    )(h, gain, w_gate, w_up, w_down)


def _in_body(h_ref, g_ref, w_ref, qkg_ref, cos_ref, sin_ref, seg_ref,
             q_ref, k_ref, vt_ref, cb_ref, ucv_ref, *, tm):
    xn = (_rms(h_ref[...]) * g_ref[...]).astype(BF16)
    u_all = jnp.dot(xn, w_ref[...], preferred_element_type=F32)
    cos = cos_ref[...]
    sin = sin_ref[...]
    for c in range(2 * N_KV_HEADS):
        x0 = u_all[:, c * GROUP_LANES:c * GROUP_LANES + HALF_LANES]
        x1 = u_all[:, c * GROUP_LANES + HALF_LANES:(c + 1) * GROUP_LANES]
        ssq = _segment_sums(x0 * x0 + x1 * x1, seg_ref)
        inv = lax.rsqrt(ssq * (1.0 / HEAD_DIM) + EPS)
        a0 = x0 * inv * qkg_ref[c:c + 1, :HALF_LANES]
        a1 = x1 * inv * qkg_ref[c:c + 1, HALF_LANES:]
        o0 = a0 * cos - a1 * sin
        o1 = a0 * sin + a1 * cos
        out_ref = q_ref if c < N_KV_HEADS else k_ref
        out_ref[c % N_KV_HEADS, :, :HALF_LANES] = o0.astype(BF16)
        out_ref[c % N_KV_HEADS, :, HALF_LANES:] = o1.astype(BF16)
    v = u_all[:, QK_COLS:QK_COLS + KV_WIDTH]
    lane = lax.broadcasted_iota(jnp.int32, v.shape, 1)
    for g in range(N_KV_HEADS):
        v_ones = jnp.where(lane < HEAD_DIM, v if g == 0 else pltpu.roll(v, HEAD_DIM, 1), 1.0)
        for j in range(tm // LANES):
            vt_ref[g, j] = v_ones[j * LANES:(j + 1) * LANES, :].T[:VT_ROWS, :].astype(BF16)
    conv0 = QK_COLS + KV_WIDTH
    cb_ref[...] = u_all[:, conv0:conv0 + CONV_WIDTH]
    ucv_ref[...] = (u_all[:, conv0 + CONV_WIDTH:conv0 + 2 * CONV_WIDTH]
                    * u_all[:, conv0 + 2 * CONV_WIDTH:conv0 + 3 * CONV_WIDTH])


def _in_proj(h, gain, w_in, qk_gain, cos, sin, seg, lp, tm):
    rows = h.shape[0]
    tiles_per_seq = lp // tm
    return pl.pallas_call(
        functools.partial(_in_body, tm=tm),
        grid=(rows // tm,),
        in_specs=[
            pl.BlockSpec((tm, D_MODEL), lambda i: (i, 0)),
            _const_spec((1, D_MODEL)),
            _const_spec((D_MODEL, IN_COLS)),
            _const_spec((2 * N_KV_HEADS, GROUP_LANES)),
            pl.BlockSpec((tm, HALF_LANES), lambda i: (i % tiles_per_seq, 0)),
            pl.BlockSpec((tm, HALF_LANES), lambda i: (i % tiles_per_seq, 0)),
            _const_spec((2 * HALF_LANES, HALF_LANES)),
        ],
        out_specs=[
            pl.BlockSpec((N_KV_HEADS, tm, GROUP_LANES), lambda i: (0, i, 0)),
            pl.BlockSpec((N_KV_HEADS, tm, GROUP_LANES), lambda i: (0, i, 0)),
            pl.BlockSpec((N_KV_HEADS, tm // LANES, VT_ROWS, LANES), lambda i: (0, i, 0, 0)),
            pl.BlockSpec((tm, CONV_WIDTH), lambda i: (i, 0)),
            pl.BlockSpec((tm, CONV_WIDTH), lambda i: (i, 0)),
        ],
        out_shape=[
            jax.ShapeDtypeStruct((N_KV_HEADS, rows, GROUP_LANES), BF16),
            jax.ShapeDtypeStruct((N_KV_HEADS, rows, GROUP_LANES), BF16),
            jax.ShapeDtypeStruct((N_KV_HEADS, rows // LANES, VT_ROWS, LANES), BF16),
            jax.ShapeDtypeStruct((rows, CONV_WIDTH), F32),
            jax.ShapeDtypeStruct((rows, CONV_WIDTH), F32),
        ],
        compiler_params=_params(1),
        name="in_proj",
    )(h, gain, w_in, qk_gain, cos, sin, seg)


def _attn_body(q_ref, k_ref, vt_ref, gain_ref, o_ref, qt_scr, s_scr, mx_scr, m_scr, acc_scr, *, tq, n_steps):
    qt = q_ref[0].astype(F32).T
    dim = lax.broadcasted_iota(jnp.int32, qt.shape, 0)
    head_of_dim = (dim // PAIRS) % KV_GROUP
    for r in range(KV_GROUP):
        qt_scr[:, r * tq:(r + 1) * tq] = jnp.where(head_of_dim == r, qt, 0.0).astype(BF16)

    def sublane_tiles(x):
        return [x[j * SUBLANES:(j + 1) * SUBLANES, :] for j in range(x.shape[0] // SUBLANES)]

    def produce(c, slot):
        k = k_ref[0, pl.ds(pl.multiple_of(SEQ_PAD + c * KV_CHUNK, LANES), KV_CHUNK), :]
        s = jnp.dot(k, qt_scr[...], preferred_element_type=F32)
        s_scr[slot] = s
        mx_scr[slot] = functools.reduce(jnp.maximum, sublane_tiles(s))

    def accumulate(s, m_cur, vt):
        m_old = m_scr[...]
        m_new = jnp.maximum(m_old, m_cur)
        p = jnp.exp2(s - m_new).astype(BF16)
        pv = jnp.dot(vt, p, preferred_element_type=F32)
        acc_scr[...] = jnp.exp2(m_old - m_new) * acc_scr[...] + pv
        m_scr[...] = m_new

    def consume(c, slot):
        vt4 = vt_ref[0, pl.ds(1 + c * (KV_CHUNK // LANES), KV_CHUNK // LANES)]
        vt = jnp.concatenate([vt4[j] for j in range(KV_CHUNK // LANES)], axis=1)
        accumulate(s_scr[slot], jnp.max(mx_scr[slot], axis=0, keepdims=True), vt)

    m_scr[...] = jnp.full(m_scr.shape, MASKED_SCORE, F32)
    acc_scr[...] = jnp.zeros(acc_scr.shape, F32)

    produce(0, 0)

    def trip(i, carry):
        for u in range(LOOP_CHUNKS):
            produce(i * LOOP_CHUNKS + u + 1, (u + 1) % 2)
            consume(i * LOOP_CHUNKS + u, u % 2)
        return carry

    looped = (n_steps - 2) // LOOP_CHUNKS * LOOP_CHUNKS
    lax.fori_loop(0, looped // LOOP_CHUNKS, trip, 0)
    for c in range(looped, n_steps - 1):
        produce(c + 1, (c + 1) % 2)
        consume(c, c % 2)
    s = jnp.dot(k_ref[0, 0:SEQ_PAD, :], qt_scr[...], preferred_element_type=F32)
    key = lax.broadcasted_iota(jnp.int32, s.shape, 0)
    s = jnp.where(key >= SEQ_PAD - N_META, s, MASKED_SCORE)
    consume(n_steps - 1, 1)
    accumulate(s, jnp.max(s, axis=0, keepdims=True), vt_ref[0, 0])

    acc = acc_scr[...]
    is_value = lax.broadcasted_iota(jnp.int32, acc.shape, 0) < HEAD_DIM
    weight = jnp.where(is_value, 1.0 / HEAD_DIM, EPS / ONES_ROWS)
    yt = acc * lax.rsqrt(jnp.sum(acc * acc * weight, axis=0, keepdims=True))
    yt = jnp.concatenate([yt, jnp.zeros((LANES - VT_ROWS, yt.shape[1]), F32)], axis=0)
    heads = [yt[:, r * tq:(r + 1) * tq].T * gain_ref[0, r:r + 1, :] for r in range(KV_GROUP)]
    is_value = lax.broadcasted_iota(jnp.int32, (tq, LANES), 1) < HEAD_DIM
    o_ref[0, :, :LANES] = jnp.where(is_value, heads[0], pltpu.roll(heads[1], HEAD_DIM, 1)).astype(BF16)
    o_ref[0, :, LANES:] = jnp.where(is_value, heads[2], pltpu.roll(heads[3], HEAD_DIM, 1)).astype(BF16)


def _attention(q, k, vt, gain, batch, lp, tq):
    rows = batch * lp
    q_tiles = lp // tq
    chunks = lp // LANES
    n_steps = (lp - SEQ_PAD) // KV_CHUNK
    return pl.pallas_call(
        functools.partial(_attn_body, tq=tq, n_steps=n_steps),
        grid=(batch, N_KV_HEADS, q_tiles),
        in_specs=[
            pl.BlockSpec((1, tq, GROUP_LANES), lambda b, g, i: (g, b * q_tiles + i, 0)),
            pl.BlockSpec((1, lp, GROUP_LANES), lambda b, g, i: (g, b, 0)),
            pl.BlockSpec((1, chunks, VT_ROWS, LANES), lambda b, g, i: (g, b, 0, 0)),
            pl.BlockSpec((1, KV_GROUP, LANES), lambda b, g, i: (g, 0, 0)),
        ],
        out_specs=pl.BlockSpec((1, tq, GROUP_LANES), lambda b, g, i: (g, b * q_tiles + i, 0)),
        out_shape=jax.ShapeDtypeStruct((N_KV_HEADS, rows, GROUP_LANES), BF16),
        scratch_shapes=[
            pltpu.VMEM((GROUP_LANES, KV_GROUP * tq), BF16),
            pltpu.VMEM((2, KV_CHUNK, KV_GROUP * tq), F32),
            pltpu.VMEM((2, SUBLANES, KV_GROUP * tq), F32),
            pltpu.VMEM((1, KV_GROUP * tq), F32),
            pltpu.VMEM((VT_ROWS, KV_GROUP * tq), F32),
        ],
        compiler_params=_params(3),
        name="attn",
    )(q, k, vt, gain)


def _out_body(ya_ref, cb_ref, ucv_ref, prev_ref, next_ref, h_ref, cw_ref, cbias_ref, cgain_ref,
              wo_ref, seg_ref, o_ref, ext_scr, mix_scr, *, tm, lp):
    ext_scr[0:SUBLANES, :] = prev_ref[...]
    ext_scr[SUBLANES:SUBLANES + tm, :] = ucv_ref[...]
    ext_scr[SUBLANES + tm:, :] = next_ref[...]
    pos = (pl.program_id(0) % (lp // tm)) * tm + lax.broadcasted_iota(jnp.int32, (tm, 1), 0)
    left = jnp.where(pos == SEQ_PAD - N_META, 0.0, ext_scr[SUBLANES - 1:SUBLANES - 1 + tm, :])
    right = jnp.where(pos == lp - 1, 0.0, ext_scr[SUBLANES + 1:SUBLANES + 1 + tm, :])
    conv = left * cw_ref[0:1, :] + ucv_ref[...] * cw_ref[1:2, :] + right * cw_ref[2:3, :] + cbias_ref[...]
    y = cb_ref[...] * conv
    for g in range(N_KV_HEADS):
        mix_scr[:, g * GROUP_LANES:(g + 1) * GROUP_LANES] = ya_ref[g]
    for half in range(CONV_WIDTH // MXU_WIDTH):
        cols = slice(half * MXU_WIDTH, (half + 1) * MXU_WIDTH)
        yh = y[:, cols]
        inv = lax.rsqrt(_segment_sums(yh * yh, seg_ref) * (1.0 / HEAD_DIM) + EPS)
        mix_scr[:, ATTN_WIDTH + half * MXU_WIDTH:ATTN_WIDTH + (half + 1) * MXU_WIDTH] = (
            yh * inv * cgain_ref[:, cols]).astype(BF16)
    o_ref[...] = h_ref[...] + jnp.dot(mix_scr[...], wo_ref[...], preferred_element_type=F32)


def _out_proj(ya, cb, ucv, h, conv_w, conv_b, conv_gain, w_out, seg, lp, tm):
    rows = h.shape[0]
    halo_per_tile = tm // SUBLANES
    last_halo = rows // SUBLANES - 1
    return pl.pallas_call(
        functools.partial(_out_body, tm=tm, lp=lp),
        grid=(rows // tm,),
        in_specs=[
            pl.BlockSpec((N_KV_HEADS, tm, GROUP_LANES), lambda i: (0, i, 0)),
            pl.BlockSpec((tm, CONV_WIDTH), lambda i: (i, 0)),
            pl.BlockSpec((tm, CONV_WIDTH), lambda i: (i, 0)),
            pl.BlockSpec((SUBLANES, CONV_WIDTH), lambda i: (jnp.maximum(i * halo_per_tile - 1, 0), 0)),
            pl.BlockSpec((SUBLANES, CONV_WIDTH), lambda i: (jnp.minimum((i + 1) * halo_per_tile, last_halo), 0)),
            pl.BlockSpec((tm, D_MODEL), lambda i: (i, 0)),
            _const_spec((3, CONV_WIDTH)),
            _const_spec((1, CONV_WIDTH)),
            _const_spec((1, CONV_WIDTH)),
            _const_spec((D_MODEL, D_MODEL)),
            _const_spec((2 * MXU_WIDTH, MXU_WIDTH)),
        ],
        out_specs=pl.BlockSpec((tm, D_MODEL), lambda i: (i, 0)),
        out_shape=jax.ShapeDtypeStruct((rows, D_MODEL), F32),
        scratch_shapes=[pltpu.VMEM((tm + 2 * SUBLANES, CONV_WIDTH), F32), pltpu.VMEM((tm, D_MODEL), BF16)],
        compiler_params=_params(1),
        name="out_proj",
    )(ya, cb, ucv, ucv, ucv, h, conv_w, conv_b, conv_gain, w_out, seg)


def _final_body(h_ref, g_ref, o_ref):
    o_ref[0] = _rms(h_ref[...]) * g_ref[...]


def _final_norm(h, gain, batch, n, tm):
    lp = SEQ_PAD + n
    tiles = n // tm
    return pl.pallas_call(
        _final_body,
        grid=(batch, tiles),
        in_specs=[
            pl.BlockSpec((pl.Element(tm), pl.Element(D_MODEL)), lambda b, i: (pl.multiple_of(b * lp + SEQ_PAD + i * tm, LANES), 0)),
            _const_spec((1, D_MODEL)),
        ],
        out_specs=pl.BlockSpec((1, tm, D_MODEL), lambda b, i: (b, i, 0)),
        out_shape=jax.ShapeDtypeStruct((batch, n, D_MODEL), F32),
        compiler_params=_params(2),
        name="final_norm",
    )(h, gain)


def _largest_tile(extent, cap):
    units = extent // LANES
    best = max(d for d in range(1, units + 1) if units % d == 0 and d * LANES <= cap)
    return best * LANES


def _in_column_order():
    halves_heads_pairs = [(half, r, i) for half in range(2) for r in range(KV_GROUP) for i in range(PAIRS)]
    cols = []
    for g in range(N_KV_HEADS):
        cols += [(KV_GROUP * g + r) * HEAD_DIM + 2 * i + half for half, r, i in halves_heads_pairs]
    for g in range(N_KV_HEADS):
        cols += [ATTN_WIDTH + g * HEAD_DIM + 2 * i + half for half, r, i in halves_heads_pairs]
    cols += list(range(ATTN_WIDTH + KV_WIDTH, IN_WIDTH))
    return np.asarray(cols, np.int32)


def _head_gain_order():
    return np.asarray([2 * i + half for half in range(2) for r in range(KV_GROUP) for i in range(PAIRS)], np.int32)


def _rope_tables(n):
    rows = n // GRID_W
    row = jnp.repeat(jnp.arange(rows, dtype=F32), GRID_W)
    col = jnp.tile(jnp.arange(GRID_W, dtype=F32), rows)
    row = jnp.concatenate([jnp.zeros((SEQ_PAD,), F32), row])
    col = jnp.concatenate([jnp.zeros((SEQ_PAD,), F32), col])
    freqs = ROPE_THETA ** (-jnp.arange(ROPE_PAIRS_AXIS, dtype=F32) / ROPE_PAIRS_AXIS)
    ang = jnp.concatenate([row[:, None] * freqs, col[:, None] * freqs], axis=-1)
    return jnp.tile(jnp.cos(ang), (1, KV_GROUP)), jnp.tile(jnp.sin(ang), (1, KV_GROUP))


def _stacked_block_diag_ones(size, block):
    idx = np.arange(size) // block
    seg = idx[:, None] == idx[None, :]
    return jnp.asarray(np.concatenate([seg, seg], axis=0), BF16)


def _trunk(x, meta_tokens, p):
    batch, n, d = x.shape
    assert d == D_MODEL and n % (2 * KV_CHUNK) == 0 and n % GRID_W == 0
    lp = SEQ_PAD + n
    rows = batch * lp
    tm_ffn = _largest_tile(rows, 512)
    tm_seq = _largest_tile(lp, 768)
    tq = tm_seq
    h = jnp.concatenate([
        jnp.zeros((batch, SEQ_PAD - N_META, d), F32),
        jnp.broadcast_to(meta_tokens[None].astype(F32), (batch, N_META, d)),
        x.astype(F32)], axis=1).reshape(rows, d)
    cos, sin = _rope_tables(n)
    for l in range(p["depth"]):
        h = _ffn(h, p["ffn1_norm"][l], p["ffn1_w_gate"][l], p["ffn1_w_up"][l], p["ffn1_w_down"][l], tm_ffn)
        q, k, vt, cb, ucv = _in_proj(h, p["mix_norm"][l], p["w_in"][l], p["qk_gain"][l], cos, sin,
                                     p["seg_pairs"], lp, tm_seq)
        ya = _attention(q, k, vt, p["attn_gain"][l], batch, lp, tq)
        h = _out_proj(ya, cb, ucv, h, p["conv_w"][l], p["conv_b"][l], p["conv_gain"][l], p["w_out"][l],
                      p["seg_heads"], lp, tm_seq)
        h = _ffn(h, p["ffn2_norm"][l], p["ffn2_w_gate"][l], p["ffn2_w_up"][l], p["ffn2_w_down"][l], tm_ffn)
    return _final_norm(h, p["final_norm"], batch, n, 2 * KV_CHUNK)


def kernel(x_prompt, x_sample, meta_tokens, ffn1_norm, ffn1_w_gate, ffn1_w_up, ffn1_w_down, mix_norm, w_in, q_norm, k_norm, conv_w, conv_b, attn_out_norm, conv_out_norm, w_out, ffn2_norm, ffn2_w_gate, ffn2_w_up, ffn2_w_down, final_norm):
    depth = w_in.shape[0]
    order = _head_gain_order()
    q_gain = q_norm[:, order] * (HEAD_DIM ** -0.5 * LOG2_E)
    k_gain = k_norm[:, order]
    p = {
        "depth": depth,
        "ffn1_norm": ffn1_norm[:, None, :],
        "ffn1_w_gate": ffn1_w_gate.astype(BF16),
        "ffn1_w_up": ffn1_w_up.astype(BF16),
        "ffn1_w_down": ffn1_w_down.astype(BF16),
        "mix_norm": mix_norm[:, None, :],
        "w_in": w_in[:, :, _in_column_order()].astype(BF16),
        "qk_gain": jnp.stack([q_gain, q_gain, k_gain, k_gain], axis=1),
        "attn_gain": jnp.pad(attn_out_norm.reshape(depth, N_KV_HEADS, KV_GROUP, HEAD_DIM),
                             ((0, 0), (0, 0), (0, 0), (0, LANES - HEAD_DIM))).reshape(depth, N_KV_HEADS, KV_GROUP, LANES),
        "conv_w": conv_w,
        "conv_b": conv_b[:, None, :],
        "conv_gain": conv_out_norm[:, None, :],
        "w_out": w_out.astype(BF16),
        "ffn2_norm": ffn2_norm[:, None, :],
        "ffn2_w_gate": ffn2_w_gate.astype(BF16),
        "ffn2_w_up": ffn2_w_up.astype(BF16),
        "ffn2_w_down": ffn2_w_down.astype(BF16),
        "final_norm": final_norm[None, :],
        "seg_pairs": _stacked_block_diag_ones(HALF_LANES, PAIRS),
        "seg_heads": _stacked_block_diag_ones(MXU_WIDTH, HEAD_DIM),
    }
    return (_trunk(x_prompt, meta_tokens, p), _trunk(x_sample, meta_tokens, p))
```

```python
import functools

import numpy as np
import jax
import jax.numpy as jnp
from jax import lax
from jax.experimental import pallas as pl
from jax.experimental.pallas import tpu as pltpu

F32 = jnp.float32
BF16 = jnp.bfloat16

D_MODEL = 1024
N_META = 16
GRID_W = 64
HEAD_DIM = 64
N_Q_HEADS = 8
N_KV_HEADS = 2
KV_GROUP = N_Q_HEADS // N_KV_HEADS
ATTN_WIDTH = N_Q_HEADS * HEAD_DIM
KV_WIDTH = N_KV_HEADS * HEAD_DIM
CONV_WIDTH = D_MODEL - ATTN_WIDTH
D_FF = 2816
ROPE_THETA = 10000.0
ROPE_PAIRS_AXIS = HEAD_DIM // 4
EPS = 1e-6
IN_WIDTH = ATTN_WIDTH + 2 * KV_WIDTH + 3 * CONV_WIDTH

LANES = 128
SUBLANES = 8
MXU_WIDTH = 256
SEQ_PAD = LANES
GROUP_LANES = KV_GROUP * HEAD_DIM
HALF_LANES = GROUP_LANES // 2
PAIRS = HEAD_DIM // 2
QK_COLS = 2 * N_KV_HEADS * GROUP_LANES
IN_COLS = QK_COLS + KV_WIDTH + 3 * CONV_WIDTH
ONES_ROWS = 16
VT_ROWS = HEAD_DIM + ONES_ROWS
KV_CHUNK = 4 * LANES
LOOP_CHUNKS = 4
COL_BLOCK = MXU_WIDTH
FF_CHUNKS = ((0, 768), (768, 768), (1536, 768), (2304, 512))
MASKED_SCORE = -1e30
LOG2_E = 1.4426950408889634
VMEM_LIMIT = 56 * 1024 * 1024


def _const_spec(shape):
    zeros = (0,) * len(shape)
    return pl.BlockSpec(shape, lambda *_: zeros, pipeline_mode=pl.Buffered(1))


def _params(n_axes):
    return pltpu.CompilerParams(dimension_semantics=("arbitrary",) * n_axes, vmem_limit_bytes=VMEM_LIMIT)


def _rms(x):
    return x * lax.rsqrt(jnp.mean(x * x, axis=-1, keepdims=True) + EPS)


def _segment_sums(x, seg_ref):
    hi = x.astype(BF16)
    lo = (x - hi.astype(F32)).astype(BF16)
    return jnp.dot(jnp.concatenate([hi, lo], axis=1), seg_ref[...], preferred_element_type=F32)


def _ffn_body(h_ref, g_ref, wg_ref, wu_ref, wd_ref, o_ref, act_scr):
    x = h_ref[...]
    xn = (_rms(x) * g_ref[...]).astype(BF16)
    for start, size in FF_CHUNKS:
        gate = jnp.dot(xn, wg_ref[:, start:start + size], preferred_element_type=F32)
        up = jnp.dot(xn, wu_ref[:, start:start + size], preferred_element_type=F32)
        silu = gate * (1.0 / (1.0 + jnp.exp(-gate)))
        act_scr[:, start:start + size] = (silu * up).astype(BF16)
    o_ref[...] = x + 0.5 * jnp.dot(act_scr[...], wd_ref[...], preferred_element_type=F32)


def _ffn(h, gain, w_gate, w_up, w_down, tm):
    rows = h.shape[0]
    return pl.pallas_call(
        _ffn_body,
        grid=(rows // tm,),
        in_specs=[
            pl.BlockSpec((tm, D_MODEL), lambda i: (i, 0)),
            _const_spec((1, D_MODEL)),
            _const_spec((D_MODEL, D_FF)),
            _const_spec((D_MODEL, D_FF)),
            _const_spec((D_FF, D_MODEL)),
        ],
        out_specs=pl.BlockSpec((tm, D_MODEL), lambda i: (i, 0)),
        out_shape=jax.ShapeDtypeStruct((rows, D_MODEL), F32),
        scratch_shapes=[pltpu.VMEM((tm, D_FF), BF16)],
        compiler_params=_params(1),
        name="ffn",
    )(h, gain, w_gate, w_up, w_down)


def _in_body(h_ref, g_ref, w_ref, qkg_ref, cos_ref, sin_ref, seg_ref,
             q_ref, k_ref, vt_ref, cb_ref, ucv_ref, *, tm):
    xn = (_rms(h_ref[...]) * g_ref[...]).astype(BF16)
    u_all = jnp.dot(xn, w_ref[...], preferred_element_type=F32)
    cos = cos_ref[...]
    sin = sin_ref[...]
    for c in range(2 * N_KV_HEADS):
        x0 = u_all[:, c * GROUP_LANES:c * GROUP_LANES + HALF_LANES]
        x1 = u_all[:, c * GROUP_LANES + HALF_LANES:(c + 1) * GROUP_LANES]
        ssq = _segment_sums(x0 * x0 + x1 * x1, seg_ref)
        inv = lax.rsqrt(ssq * (1.0 / HEAD_DIM) + EPS)
        a0 = x0 * inv * qkg_ref[c:c + 1, :HALF_LANES]
        a1 = x1 * inv * qkg_ref[c:c + 1, HALF_LANES:]
        o0 = a0 * cos - a1 * sin
        o1 = a0 * sin + a1 * cos
        out_ref = q_ref if c < N_KV_HEADS else k_ref
        out_ref[c % N_KV_HEADS, :, :HALF_LANES] = o0.astype(BF16)
        out_ref[c % N_KV_HEADS, :, HALF_LANES:] = o1.astype(BF16)
    v = u_all[:, QK_COLS:QK_COLS + KV_WIDTH]
    lane = lax.broadcasted_iota(jnp.int32, v.shape, 1)
    for g in range(N_KV_HEADS):
        v_ones = jnp.where(lane < HEAD_DIM, v if g == 0 else pltpu.roll(v, HEAD_DIM, 1), 1.0)
        for j in range(tm // LANES):
            vt_ref[g, j] = v_ones[j * LANES:(j + 1) * LANES, :].T[:VT_ROWS, :].astype(BF16)
    conv0 = QK_COLS + KV_WIDTH
    cb_ref[...] = u_all[:, conv0:conv0 + CONV_WIDTH]
    ucv_ref[...] = (u_all[:, conv0 + CONV_WIDTH:conv0 + 2 * CONV_WIDTH]
                    * u_all[:, conv0 + 2 * CONV_WIDTH:conv0 + 3 * CONV_WIDTH])


def _in_proj(h, gain, w_in, qk_gain, cos, sin, seg, lp, tm):
    rows = h.shape[0]
    tiles_per_seq = lp // tm
    return pl.pallas_call(
        functools.partial(_in_body, tm=tm),
        grid=(rows // tm,),
        in_specs=[
            pl.BlockSpec((tm, D_MODEL), lambda i: (i, 0)),
            _const_spec((1, D_MODEL)),
            _const_spec((D_MODEL, IN_COLS)),
            _const_spec((2 * N_KV_HEADS, GROUP_LANES)),
            pl.BlockSpec((tm, HALF_LANES), lambda i: (i % tiles_per_seq, 0)),
            pl.BlockSpec((tm, HALF_LANES), lambda i: (i % tiles_per_seq, 0)),
            _const_spec((2 * HALF_LANES, HALF_LANES)),
        ],
        out_specs=[
            pl.BlockSpec((N_KV_HEADS, tm, GROUP_LANES), lambda i: (0, i, 0)),
            pl.BlockSpec((N_KV_HEADS, tm, GROUP_LANES), lambda i: (0, i, 0)),
            pl.BlockSpec((N_KV_HEADS, tm // LANES, VT_ROWS, LANES), lambda i: (0, i, 0, 0)),
            pl.BlockSpec((tm, CONV_WIDTH), lambda i: (i, 0)),
            pl.BlockSpec((tm, CONV_WIDTH), lambda i: (i, 0)),
        ],
        out_shape=[
            jax.ShapeDtypeStruct((N_KV_HEADS, rows, GROUP_LANES), BF16),
            jax.ShapeDtypeStruct((N_KV_HEADS, rows, GROUP_LANES), BF16),
            jax.ShapeDtypeStruct((N_KV_HEADS, rows // LANES, VT_ROWS, LANES), BF16),
            jax.ShapeDtypeStruct((rows, CONV_WIDTH), F32),
            jax.ShapeDtypeStruct((rows, CONV_WIDTH), F32),
        ],
        compiler_params=_params(1),
        name="in_proj",
    )(h, gain, w_in, qk_gain, cos, sin, seg)


def _attn_body(q_ref, k_ref, vt_ref, gain_ref, o_ref, qt_scr, s_scr, mx_scr, m_scr, acc_scr, *, tq, n_steps):
    qt = q_ref[0].astype(F32).T
    dim = lax.broadcasted_iota(jnp.int32, qt.shape, 0)
    head_of_dim = (dim // PAIRS) % KV_GROUP
    for r in range(KV_GROUP):
        qt_scr[:, r * tq:(r + 1) * tq] = jnp.where(head_of_dim == r, qt, 0.0).astype(BF16)

    def sublane_tiles(x):
        return [x[j * SUBLANES:(j + 1) * SUBLANES, :] for j in range(x.shape[0] // SUBLANES)]

    col_blocks = [slice(b * COL_BLOCK, (b + 1) * COL_BLOCK) for b in range(KV_GROUP * tq // COL_BLOCK)]

    def produce(c, slot, cols):
        k = k_ref[0, pl.ds(pl.multiple_of(SEQ_PAD + c * KV_CHUNK, LANES), KV_CHUNK), :]
        s = jnp.dot(k, qt_scr[:, cols], preferred_element_type=F32)
        s_scr[slot, :, cols] = s
        mx_scr[slot, :, cols] = functools.reduce(jnp.maximum, sublane_tiles(s))

    def accumulate(s, m_cur, vt, cols):
        m_old = m_scr[:, cols]
        m_new = jnp.maximum(m_old, m_cur)
        p = jnp.exp2(s - m_new).astype(BF16)
        pv = jnp.dot(vt, p, preferred_element_type=F32)
        acc_scr[:, cols] = jnp.exp2(m_old - m_new) * acc_scr[:, cols] + pv
        m_scr[:, cols] = m_new

    def consume(c, slot, cols):
        vt4 = vt_ref[0, pl.ds(1 + c * (KV_CHUNK // LANES), KV_CHUNK // LANES)]
        vt = jnp.concatenate([vt4[j] for j in range(KV_CHUNK // LANES)], axis=1)
        accumulate(s_scr[slot, :, cols], jnp.max(mx_scr[slot, :, cols], axis=0, keepdims=True), vt, cols)

    def stage(produced, consumed):
        for cols in col_blocks:
            if produced is not None:
                produce(produced[0], produced[1], cols)
            if consumed is not None:
                consume(consumed[0], consumed[1], cols)

    m_scr[...] = jnp.full(m_scr.shape, MASKED_SCORE, F32)
    acc_scr[...] = jnp.zeros(acc_scr.shape, F32)

    stage((0, 0), None)

    def trip(i, carry):
        for u in range(LOOP_CHUNKS):
            stage((i * LOOP_CHUNKS + u + 1, (u + 1) % 2), (i * LOOP_CHUNKS + u, u % 2))
        return carry

    looped = (n_steps - 2) // LOOP_CHUNKS * LOOP_CHUNKS
    lax.fori_loop(0, looped // LOOP_CHUNKS, trip, 0)
    for c in range(looped, n_steps - 1):
        stage((c + 1, (c + 1) % 2), (c, c % 2))
    s = jnp.dot(k_ref[0, 0:SEQ_PAD, :], qt_scr[...], preferred_element_type=F32)
    key = lax.broadcasted_iota(jnp.int32, s.shape, 0)
    s = jnp.where(key >= SEQ_PAD - N_META, s, MASKED_SCORE)
    stage(None, (n_steps - 1, 1))
    for cols in col_blocks:
        accumulate(s[:, cols], jnp.max(s[:, cols], axis=0, keepdims=True), vt_ref[0, 0], cols)

    acc = acc_scr[...]
    is_value = lax.broadcasted_iota(jnp.int32, acc.shape, 0) < HEAD_DIM
    weight = jnp.where(is_value, 1.0 / HEAD_DIM, EPS / ONES_ROWS)
    yt = acc * lax.rsqrt(jnp.sum(acc * acc * weight, axis=0, keepdims=True))
    yt = jnp.concatenate([yt, jnp.zeros((LANES - VT_ROWS, yt.shape[1]), F32)], axis=0)
    heads = [yt[:, r * tq:(r + 1) * tq].T * gain_ref[0, r:r + 1, :] for r in range(KV_GROUP)]
    is_value = lax.broadcasted_iota(jnp.int32, (tq, LANES), 1) < HEAD_DIM
    o_ref[0, :, :LANES] = jnp.where(is_value, heads[0], pltpu.roll(heads[1], HEAD_DIM, 1)).astype(BF16)
    o_ref[0, :, LANES:] = jnp.where(is_value, heads[2], pltpu.roll(heads[3], HEAD_DIM, 1)).astype(BF16)


def _attention(q, k, vt, gain, batch, lp, tq):
    rows = batch * lp
    q_tiles = lp // tq
    chunks = lp // LANES
    n_steps = (lp - SEQ_PAD) // KV_CHUNK
    return pl.pallas_call(
        functools.partial(_attn_body, tq=tq, n_steps=n_steps),
        grid=(batch, N_KV_HEADS, q_tiles),
        in_specs=[
            pl.BlockSpec((1, tq, GROUP_LANES), lambda b, g, i: (g, b * q_tiles + i, 0)),
            pl.BlockSpec((1, lp, GROUP_LANES), lambda b, g, i: (g, b, 0)),
            pl.BlockSpec((1, chunks, VT_ROWS, LANES), lambda b, g, i: (g, b, 0, 0)),
            pl.BlockSpec((1, KV_GROUP, LANES), lambda b, g, i: (g, 0, 0)),
        ],
        out_specs=pl.BlockSpec((1, tq, GROUP_LANES), lambda b, g, i: (g, b * q_tiles + i, 0)),
        out_shape=jax.ShapeDtypeStruct((N_KV_HEADS, rows, GROUP_LANES), BF16),
        scratch_shapes=[
            pltpu.VMEM((GROUP_LANES, KV_GROUP * tq), BF16),
            pltpu.VMEM((2, KV_CHUNK, KV_GROUP * tq), F32),
            pltpu.VMEM((2, SUBLANES, KV_GROUP * tq), F32),
            pltpu.VMEM((1, KV_GROUP * tq), F32),
            pltpu.VMEM((VT_ROWS, KV_GROUP * tq), F32),
        ],
        compiler_params=_params(3),
        name="attn",
    )(q, k, vt, gain)


def _out_body(ya_ref, cb_ref, ucv_ref, prev_ref, next_ref, h_ref, cw_ref, cbias_ref, cgain_ref,
              wo_ref, seg_ref, o_ref, ext_scr, mix_scr, *, tm, lp):
    ext_scr[0:SUBLANES, :] = prev_ref[...]
    ext_scr[SUBLANES:SUBLANES + tm, :] = ucv_ref[...]
    ext_scr[SUBLANES + tm:, :] = next_ref[...]
    pos = (pl.program_id(0) % (lp // tm)) * tm + lax.broadcasted_iota(jnp.int32, (tm, 1), 0)
    left = jnp.where(pos == SEQ_PAD - N_META, 0.0, ext_scr[SUBLANES - 1:SUBLANES - 1 + tm, :])
    right = jnp.where(pos == lp - 1, 0.0, ext_scr[SUBLANES + 1:SUBLANES + 1 + tm, :])
    conv = left * cw_ref[0:1, :] + ucv_ref[...] * cw_ref[1:2, :] + right * cw_ref[2:3, :] + cbias_ref[...]
    y = cb_ref[...] * conv
    for g in range(N_KV_HEADS):
        mix_scr[:, g * GROUP_LANES:(g + 1) * GROUP_LANES] = ya_ref[g]
    for half in range(CONV_WIDTH // MXU_WIDTH):
        cols = slice(half * MXU_WIDTH, (half + 1) * MXU_WIDTH)
        yh = y[:, cols]
        inv = lax.rsqrt(_segment_sums(yh * yh, seg_ref) * (1.0 / HEAD_DIM) + EPS)
        mix_scr[:, ATTN_WIDTH + half * MXU_WIDTH:ATTN_WIDTH + (half + 1) * MXU_WIDTH] = (
            yh * inv * cgain_ref[:, cols]).astype(BF16)
    o_ref[...] = h_ref[...] + jnp.dot(mix_scr[...], wo_ref[...], preferred_element_type=F32)


def _out_proj(ya, cb, ucv, h, conv_w, conv_b, conv_gain, w_out, seg, lp, tm):
    rows = h.shape[0]
    halo_per_tile = tm // SUBLANES
    last_halo = rows // SUBLANES - 1
    return pl.pallas_call(
        functools.partial(_out_body, tm=tm, lp=lp),
        grid=(rows // tm,),
        in_specs=[
            pl.BlockSpec((N_KV_HEADS, tm, GROUP_LANES), lambda i: (0, i, 0)),
            pl.BlockSpec((tm, CONV_WIDTH), lambda i: (i, 0)),
            pl.BlockSpec((tm, CONV_WIDTH), lambda i: (i, 0)),
            pl.BlockSpec((SUBLANES, CONV_WIDTH), lambda i: (jnp.maximum(i * halo_per_tile - 1, 0), 0)),
            pl.BlockSpec((SUBLANES, CONV_WIDTH), lambda i: (jnp.minimum((i + 1) * halo_per_tile, last_halo), 0)),
            pl.BlockSpec((tm, D_MODEL), lambda i: (i, 0)),
            _const_spec((3, CONV_WIDTH)),
            _const_spec((1, CONV_WIDTH)),
            _const_spec((1, CONV_WIDTH)),
            _const_spec((D_MODEL, D_MODEL)),
            _const_spec((2 * MXU_WIDTH, MXU_WIDTH)),
        ],
        out_specs=pl.BlockSpec((tm, D_MODEL), lambda i: (i, 0)),
        out_shape=jax.ShapeDtypeStruct((rows, D_MODEL), F32),
        scratch_shapes=[pltpu.VMEM((tm + 2 * SUBLANES, CONV_WIDTH), F32), pltpu.VMEM((tm, D_MODEL), BF16)],
        compiler_params=_params(1),
        name="out_proj",
    )(ya, cb, ucv, ucv, ucv, h, conv_w, conv_b, conv_gain, w_out, seg)


def _final_body(h_ref, g_ref, o_ref):
    o_ref[0] = _rms(h_ref[...]) * g_ref[...]


def _final_norm(h, gain, batch, n, tm):
    lp = SEQ_PAD + n
    tiles = n // tm
    return pl.pallas_call(
        _final_body,
        grid=(batch, tiles),
        in_specs=[
            pl.BlockSpec((pl.Element(tm), pl.Element(D_MODEL)), lambda b, i: (pl.multiple_of(b * lp + SEQ_PAD + i * tm, LANES), 0)),
            _const_spec((1, D_MODEL)),
        ],
        out_specs=pl.BlockSpec((1, tm, D_MODEL), lambda b, i: (b, i, 0)),
        out_shape=jax.ShapeDtypeStruct((batch, n, D_MODEL), F32),
        compiler_params=_params(2),
        name="final_norm",
    )(h, gain)


def _largest_tile(extent, cap):
    units = extent // LANES
    best = max(d for d in range(1, units + 1) if units % d == 0 and d * LANES <= cap)
    return best * LANES


def _in_column_order():
    halves_heads_pairs = [(half, r, i) for half in range(2) for r in range(KV_GROUP) for i in range(PAIRS)]
    cols = []
    for g in range(N_KV_HEADS):
        cols += [(KV_GROUP * g + r) * HEAD_DIM + 2 * i + half for half, r, i in halves_heads_pairs]
    for g in range(N_KV_HEADS):
        cols += [ATTN_WIDTH + g * HEAD_DIM + 2 * i + half for half, r, i in halves_heads_pairs]
    cols += list(range(ATTN_WIDTH + KV_WIDTH, IN_WIDTH))
    return np.asarray(cols, np.int32)


def _head_gain_order():
    return np.asarray([2 * i + half for half in range(2) for r in range(KV_GROUP) for i in range(PAIRS)], np.int32)


def _rope_tables(n):
    rows = n // GRID_W
    row = jnp.repeat(jnp.arange(rows, dtype=F32), GRID_W)
    col = jnp.tile(jnp.arange(GRID_W, dtype=F32), rows)
    row = jnp.concatenate([jnp.zeros((SEQ_PAD,), F32), row])
    col = jnp.concatenate([jnp.zeros((SEQ_PAD,), F32), col])
    freqs = ROPE_THETA ** (-jnp.arange(ROPE_PAIRS_AXIS, dtype=F32) / ROPE_PAIRS_AXIS)
    ang = jnp.concatenate([row[:, None] * freqs, col[:, None] * freqs], axis=-1)
    return jnp.tile(jnp.cos(ang), (1, KV_GROUP)), jnp.tile(jnp.sin(ang), (1, KV_GROUP))


def _stacked_block_diag_ones(size, block):
    idx = np.arange(size) // block
    seg = idx[:, None] == idx[None, :]
    return jnp.asarray(np.concatenate([seg, seg], axis=0), BF16)


def _trunk(x, meta_tokens, p):
    batch, n, d = x.shape
    assert d == D_MODEL and n % (2 * KV_CHUNK) == 0 and n % GRID_W == 0
    lp = SEQ_PAD + n
    rows = batch * lp
    tm_ffn = _largest_tile(rows, 512)
    tm_seq = _largest_tile(lp, 768)
    tq = tm_seq
    h = jnp.concatenate([
        jnp.zeros((batch, SEQ_PAD - N_META, d), F32),
        jnp.broadcast_to(meta_tokens[None].astype(F32), (batch, N_META, d)),
        x.astype(F32)], axis=1).reshape(rows, d)
    cos, sin = _rope_tables(n)
    for l in range(p["depth"]):
        h = _ffn(h, p["ffn1_norm"][l], p["ffn1_w_gate"][l], p["ffn1_w_up"][l], p["ffn1_w_down"][l], tm_ffn)
        q, k, vt, cb, ucv = _in_proj(h, p["mix_norm"][l], p["w_in"][l], p["qk_gain"][l], cos, sin,
                                     p["seg_pairs"], lp, tm_seq)
        ya = _attention(q, k, vt, p["attn_gain"][l], batch, lp, tq)
        h = _out_proj(ya, cb, ucv, h, p["conv_w"][l], p["conv_b"][l], p["conv_gain"][l], p["w_out"][l],
                      p["seg_heads"], lp, tm_seq)
        h = _ffn(h, p["ffn2_norm"][l], p["ffn2_w_gate"][l], p["ffn2_w_up"][l], p["ffn2_w_down"][l], tm_ffn)
    return _final_norm(h, p["final_norm"], batch, n, 2 * KV_CHUNK)


def kernel(x_prompt, x_sample, meta_tokens, ffn1_norm, ffn1_w_gate, ffn1_w_up, ffn1_w_down, mix_norm, w_in, q_norm, k_norm, conv_w, conv_b, attn_out_norm, conv_out_norm, w_out, ffn2_norm, ffn2_w_gate, ffn2_w_up, ffn2_w_down, final_norm):
    depth = w_in.shape[0]
    order = _head_gain_order()
    q_gain = q_norm[:, order] * (HEAD_DIM ** -0.5 * LOG2_E)
    k_gain = k_norm[:, order]
    p = {
        "depth": depth,
        "ffn1_norm": ffn1_norm[:, None, :],
        "ffn1_w_gate": ffn1_w_gate.astype(BF16),
        "ffn1_w_up": ffn1_w_up.astype(BF16),
        "ffn1_w_down": ffn1_w_down.astype(BF16),
        "mix_norm": mix_norm[:, None, :],
        "w_in": w_in[:, :, _in_column_order()].astype(BF16),
        "qk_gain": jnp.stack([q_gain, q_gain, k_gain, k_gain], axis=1),
        "attn_gain": jnp.pad(attn_out_norm.reshape(depth, N_KV_HEADS, KV_GROUP, HEAD_DIM),
                             ((0, 0), (0, 0), (0, 0), (0, LANES - HEAD_DIM))).reshape(depth, N_KV_HEADS, KV_GROUP, LANES),
        "conv_w": conv_w,
        "conv_b": conv_b[:, None, :],
        "conv_gain": conv_out_norm[:, None, :],
        "w_out": w_out.astype(BF16),
        "ffn2_norm": ffn2_norm[:, None, :],
        "ffn2_w_gate": ffn2_w_gate.astype(BF16),
        "ffn2_w_up": ffn2_w_up.astype(BF16),
        "ffn2_w_down": ffn2_w_down.astype(BF16),
        "final_norm": final_norm[None, :],
        "seg_pairs": _stacked_block_diag_ones(HALF_LANES, PAIRS),
        "seg_heads": _stacked_block_diag_ones(MXU_WIDTH, HEAD_DIM),
    }
    return (_trunk(x_prompt, meta_tokens, p), _trunk(x_sample, meta_tokens, p))
```

```python
import functools

import numpy as np
import jax
import jax.numpy as jnp
from jax import lax
from jax.experimental import pallas as pl
from jax.experimental.pallas import tpu as pltpu

F32 = jnp.float32
BF16 = jnp.bfloat16

D_MODEL = 1024
N_META = 16
GRID_W = 64
HEAD_DIM = 64
N_Q_HEADS = 8
N_KV_HEADS = 2
KV_GROUP = N_Q_HEADS // N_KV_HEADS
ATTN_WIDTH = N_Q_HEADS * HEAD_DIM
KV_WIDTH = N_KV_HEADS * HEAD_DIM
CONV_WIDTH = D_MODEL - ATTN_WIDTH
D_FF = 2816
ROPE_THETA = 10000.0
ROPE_PAIRS_AXIS = HEAD_DIM // 4
EPS = 1e-6
IN_WIDTH = ATTN_WIDTH + 2 * KV_WIDTH + 3 * CONV_WIDTH

LANES = 128
SUBLANES = 8
MXU_WIDTH = 256
SEQ_PAD = LANES
GROUP_LANES = KV_GROUP * HEAD_DIM
HALF_LANES = GROUP_LANES // 2
PAIRS = HEAD_DIM // 2
QK_COLS = 2 * N_KV_HEADS * GROUP_LANES
IN_COLS = QK_COLS + KV_WIDTH + 3 * CONV_WIDTH
ONES_ROWS = 16
VT_ROWS = HEAD_DIM + ONES_ROWS
KV_CHUNK = 4 * LANES
LOOP_CHUNKS = 4
COL_BLOCK = MXU_WIDTH
OUT_ROW_BLOCKS = 4
FF_CHUNKS = ((0, 768), (768, 768), (1536, 768), (2304, 512))
MASKED_SCORE = -1e30
LOG2_E = 1.4426950408889634
VMEM_LIMIT = 56 * 1024 * 1024


def _const_spec(shape):
    zeros = (0,) * len(shape)
    return pl.BlockSpec(shape, lambda *_: zeros, pipeline_mode=pl.Buffered(1))


def _params(n_axes):
    return pltpu.CompilerParams(dimension_semantics=("arbitrary",) * n_axes, vmem_limit_bytes=VMEM_LIMIT)


def _rms(x):
    return x * lax.rsqrt(jnp.mean(x * x, axis=-1, keepdims=True) + EPS)


def _segment_sums(x, seg_ref):
    hi = x.astype(BF16)
    lo = (x - hi.astype(F32)).astype(BF16)
    return jnp.dot(jnp.concatenate([hi, lo], axis=1), seg_ref[...], preferred_element_type=F32)


def _ffn_body(h_ref, g_ref, wg_ref, wu_ref, wd_ref, o_ref, act_scr):
    x = h_ref[...]
    xn = (_rms(x) * g_ref[...]).astype(BF16)
    for start, size in FF_CHUNKS:
        gate = jnp.dot(xn, wg_ref[:, start:start + size], preferred_element_type=F32)
        up = jnp.dot(xn, wu_ref[:, start:start + size], preferred_element_type=F32)
        silu = gate * (1.0 / (1.0 + jnp.exp(-gate)))
        act_scr[:, start:start + size] = (silu * up).astype(BF16)
    o_ref[...] = x + 0.5 * jnp.dot(act_scr[...], wd_ref[...], preferred_element_type=F32)


def _ffn(h, gain, w_gate, w_up, w_down, tm):
    rows = h.shape[0]
    return pl.pallas_call(
        _ffn_body,
        grid=(rows // tm,),
        in_specs=[
            pl.BlockSpec((tm, D_MODEL), lambda i: (i, 0)),
            _const_spec((1, D_MODEL)),
            _const_spec((D_MODEL, D_FF)),
            _const_spec((D_MODEL, D_FF)),
            _const_spec((D_FF, D_MODEL)),
        ],
        out_specs=pl.BlockSpec((tm, D_MODEL), lambda i: (i, 0)),
        out_shape=jax.ShapeDtypeStruct((rows, D_MODEL), F32),
        scratch_shapes=[pltpu.VMEM((tm, D_FF), BF16)],
        compiler_params=_params(1),
        name="ffn",
    )(h, gain, w_gate, w_up, w_down)


def _in_body(h_ref, g_ref, w_ref, qkg_ref, cos_ref, sin_ref, seg_ref,
             q_ref, k_ref, vt_ref, cb_ref, ucv_ref, *, tm):
    xn = (_rms(h_ref[...]) * g_ref[...]).astype(BF16)
    u_all = jnp.dot(xn, w_ref[...], preferred_element_type=F32)
    cos = cos_ref[...]
    sin = sin_ref[...]
    for c in range(2 * N_KV_HEADS):
        x0 = u_all[:, c * GROUP_LANES:c * GROUP_LANES + HALF_LANES]
        x1 = u_all[:, c * GROUP_LANES + HALF_LANES:(c + 1) * GROUP_LANES]
        ssq = _segment_sums(x0 * x0 + x1 * x1, seg_ref)
        inv = lax.rsqrt(ssq * (1.0 / HEAD_DIM) + EPS)
        a0 = x0 * inv * qkg_ref[c:c + 1, :HALF_LANES]
        a1 = x1 * inv * qkg_ref[c:c + 1, HALF_LANES:]
        o0 = a0 * cos - a1 * sin
        o1 = a0 * sin + a1 * cos
        out_ref = q_ref if c < N_KV_HEADS else k_ref
        out_ref[c % N_KV_HEADS, :, :HALF_LANES] = o0.astype(BF16)
        out_ref[c % N_KV_HEADS, :, HALF_LANES:] = o1.astype(BF16)
    v = u_all[:, QK_COLS:QK_COLS + KV_WIDTH]
    lane = lax.broadcasted_iota(jnp.int32, v.shape, 1)
    for g in range(N_KV_HEADS):
        v_ones = jnp.where(lane < HEAD_DIM, v if g == 0 else pltpu.roll(v, HEAD_DIM, 1), 1.0)
        for j in range(tm // LANES):
            vt_ref[g, j] = v_ones[j * LANES:(j + 1) * LANES, :].T[:VT_ROWS, :].astype(BF16)
    conv0 = QK_COLS + KV_WIDTH
    cb_ref[...] = u_all[:, conv0:conv0 + CONV_WIDTH]
    ucv_ref[...] = (u_all[:, conv0 + CONV_WIDTH:conv0 + 2 * CONV_WIDTH]
                    * u_all[:, conv0 + 2 * CONV_WIDTH:conv0 + 3 * CONV_WIDTH])


def _in_proj(h, gain, w_in, qk_gain, cos, sin, seg, lp, tm):
    rows = h.shape[0]
    tiles_per_seq = lp // tm
    return pl.pallas_call(
        functools.partial(_in_body, tm=tm),
        grid=(rows // tm,),
        in_specs=[
            pl.BlockSpec((tm, D_MODEL), lambda i: (i, 0)),
            _const_spec((1, D_MODEL)),
            _const_spec((D_MODEL, IN_COLS)),
            _const_spec((2 * N_KV_HEADS, GROUP_LANES)),
            pl.BlockSpec((tm, HALF_LANES), lambda i: (i % tiles_per_seq, 0)),
            pl.BlockSpec((tm, HALF_LANES), lambda i: (i % tiles_per_seq, 0)),
            _const_spec((2 * HALF_LANES, HALF_LANES)),
        ],
        out_specs=[
            pl.BlockSpec((N_KV_HEADS, tm, GROUP_LANES), lambda i: (0, i, 0)),
            pl.BlockSpec((N_KV_HEADS, tm, GROUP_LANES), lambda i: (0, i, 0)),
            pl.BlockSpec((N_KV_HEADS, tm // LANES, VT_ROWS, LANES), lambda i: (0, i, 0, 0)),
            pl.BlockSpec((tm, CONV_WIDTH), lambda i: (i, 0)),
            pl.BlockSpec((tm, CONV_WIDTH), lambda i: (i, 0)),
        ],
        out_shape=[
            jax.ShapeDtypeStruct((N_KV_HEADS, rows, GROUP_LANES), BF16),
            jax.ShapeDtypeStruct((N_KV_HEADS, rows, GROUP_LANES), BF16),
            jax.ShapeDtypeStruct((N_KV_HEADS, rows // LANES, VT_ROWS, LANES), BF16),
            jax.ShapeDtypeStruct((rows, CONV_WIDTH), F32),
            jax.ShapeDtypeStruct((rows, CONV_WIDTH), F32),
        ],
        compiler_params=_params(1),
        name="in_proj",
    )(h, gain, w_in, qk_gain, cos, sin, seg)


def _attn_body(q_ref, k_ref, vt_ref, gain_ref, o_ref, qt_scr, s_scr, mx_scr, m_scr, acc_scr, *, tq, n_steps):
    qt = q_ref[0].astype(F32).T
    dim = lax.broadcasted_iota(jnp.int32, qt.shape, 0)
    head_of_dim = (dim // PAIRS) % KV_GROUP
    for r in range(KV_GROUP):
        qt_scr[:, r * tq:(r + 1) * tq] = jnp.where(head_of_dim == r, qt, 0.0).astype(BF16)

    def sublane_tiles(x):
        return [x[j * SUBLANES:(j + 1) * SUBLANES, :] for j in range(x.shape[0] // SUBLANES)]

    col_blocks = [slice(b * COL_BLOCK, (b + 1) * COL_BLOCK) for b in range(KV_GROUP * tq // COL_BLOCK)]

    def produce(c, slot, cols):
        k = k_ref[0, pl.ds(pl.multiple_of(SEQ_PAD + c * KV_CHUNK, LANES), KV_CHUNK), :]
        s = jnp.dot(k, qt_scr[:, cols], preferred_element_type=F32)
        s_scr[slot, :, cols] = s
        mx_scr[slot, :, cols] = functools.reduce(jnp.maximum, sublane_tiles(s))

    def accumulate(s, m_cur, vt, cols):
        m_old = m_scr[:, cols]
        m_new = jnp.maximum(m_old, m_cur)
        p = jnp.exp2(s - m_new).astype(BF16)
        pv = jnp.dot(vt, p, preferred_element_type=F32)
        acc_scr[:, cols] = jnp.exp2(m_old - m_new) * acc_scr[:, cols] + pv
        m_scr[:, cols] = m_new

    def consume(c, slot, cols):
        vt4 = vt_ref[0, pl.ds(1 + c * (KV_CHUNK // LANES), KV_CHUNK // LANES)]
        vt = jnp.concatenate([vt4[j] for j in range(KV_CHUNK // LANES)], axis=1)
        accumulate(s_scr[slot, :, cols], jnp.max(mx_scr[slot, :, cols], axis=0, keepdims=True), vt, cols)

    def stage(produced, consumed):
        for cols in col_blocks:
            if produced is not None:
                produce(produced[0], produced[1], cols)
            if consumed is not None:
                consume(consumed[0], consumed[1], cols)

    m_scr[...] = jnp.full(m_scr.shape, MASKED_SCORE, F32)
    acc_scr[...] = jnp.zeros(acc_scr.shape, F32)

    stage((0, 0), None)

    per_trip = LOOP_CHUNKS if n_steps > 2 * LOOP_CHUNKS else 2

    def trip(i, carry):
        for u in range(per_trip):
            stage((i * per_trip + u + 1, (u + 1) % 2), (i * per_trip + u, u % 2))
        return carry

    looped = (n_steps - 2) // per_trip * per_trip
    lax.fori_loop(0, looped // per_trip, trip, 0)
    for c in range(looped, n_steps - 1):
        stage((c + 1, (c + 1) % 2), (c, c % 2))
    s = jnp.dot(k_ref[0, 0:SEQ_PAD, :], qt_scr[...], preferred_element_type=F32)
    key = lax.broadcasted_iota(jnp.int32, s.shape, 0)
    s = jnp.where(key >= SEQ_PAD - N_META, s, MASKED_SCORE)
    stage(None, (n_steps - 1, 1))
    for cols in col_blocks:
        accumulate(s[:, cols], jnp.max(s[:, cols], axis=0, keepdims=True), vt_ref[0, 0], cols)

    acc = acc_scr[...]
    is_value = lax.broadcasted_iota(jnp.int32, acc.shape, 0) < HEAD_DIM
    weight = jnp.where(is_value, 1.0 / HEAD_DIM, EPS / ONES_ROWS)
    yt = acc * lax.rsqrt(jnp.sum(acc * acc * weight, axis=0, keepdims=True))
    yt = jnp.concatenate([yt, jnp.zeros((LANES - VT_ROWS, yt.shape[1]), F32)], axis=0)
    heads = [yt[:, r * tq:(r + 1) * tq].T * gain_ref[0, r:r + 1, :] for r in range(KV_GROUP)]
    is_value = lax.broadcasted_iota(jnp.int32, (tq, LANES), 1) < HEAD_DIM
    o_ref[0, :, :LANES] = jnp.where(is_value, heads[0], pltpu.roll(heads[1], HEAD_DIM, 1)).astype(BF16)
    o_ref[0, :, LANES:] = jnp.where(is_value, heads[2], pltpu.roll(heads[3], HEAD_DIM, 1)).astype(BF16)


def _attention(q, k, vt, gain, batch, lp, tq):
    rows = batch * lp
    q_tiles = lp // tq
    chunks = lp // LANES
    n_steps = (lp - SEQ_PAD) // KV_CHUNK
    return pl.pallas_call(
        functools.partial(_attn_body, tq=tq, n_steps=n_steps),
        grid=(batch, N_KV_HEADS, q_tiles),
        in_specs=[
            pl.BlockSpec((1, tq, GROUP_LANES), lambda b, g, i: (g, b * q_tiles + i, 0)),
            pl.BlockSpec((1, lp, GROUP_LANES), lambda b, g, i: (g, b, 0)),
            pl.BlockSpec((1, chunks, VT_ROWS, LANES), lambda b, g, i: (g, b, 0, 0)),
            pl.BlockSpec((1, KV_GROUP, LANES), lambda b, g, i: (g, 0, 0)),
        ],
        out_specs=pl.BlockSpec((1, tq, GROUP_LANES), lambda b, g, i: (g, b * q_tiles + i, 0)),
        out_shape=jax.ShapeDtypeStruct((N_KV_HEADS, rows, GROUP_LANES), BF16),
        scratch_shapes=[
            pltpu.VMEM((GROUP_LANES, KV_GROUP * tq), BF16),
            pltpu.VMEM((2, KV_CHUNK, KV_GROUP * tq + LANES), F32),
            pltpu.VMEM((2, SUBLANES, KV_GROUP * tq), F32),
            pltpu.VMEM((1, KV_GROUP * tq), F32),
            pltpu.VMEM((VT_ROWS, KV_GROUP * tq), F32),
        ],
        compiler_params=_params(3),
        name="attn",
    )(q, k, vt, gain)


def _out_body(ya_ref, cb_ref, ucv_ref, prev_ref, next_ref, h_ref, cw_ref, cbias_ref, cgain_ref,
              wo_ref, seg_ref, o_ref, ext_scr, mix_scr, *, tm, lp):
    ext_scr[0:SUBLANES, :] = prev_ref[...]
    ext_scr[SUBLANES:SUBLANES + tm, :] = ucv_ref[...]
    ext_scr[SUBLANES + tm:, :] = next_ref[...]
    tile_pos = (pl.program_id(0) % (lp // tm)) * tm
    rb = tm // OUT_ROW_BLOCKS
    for blk in range(OUT_ROW_BLOCKS):
        r0 = blk * rb
        rows = slice(r0, r0 + rb)
        pos = tile_pos + r0 + lax.broadcasted_iota(jnp.int32, (rb, 1), 0)
        left = jnp.where(pos == SEQ_PAD - N_META, 0.0, ext_scr[SUBLANES - 1 + r0:SUBLANES - 1 + r0 + rb, :])
        right = jnp.where(pos == lp - 1, 0.0, ext_scr[SUBLANES + 1 + r0:SUBLANES + 1 + r0 + rb, :])
        conv = (left * cw_ref[0:1, :] + ucv_ref[rows, :] * cw_ref[1:2, :] + right * cw_ref[2:3, :]
                + cbias_ref[...])
        y = cb_ref[rows, :] * conv
        for g in range(N_KV_HEADS):
            mix_scr[rows, g * GROUP_LANES:(g + 1) * GROUP_LANES] = ya_ref[g, rows, :]
        for half in range(CONV_WIDTH // MXU_WIDTH):
            cols = slice(half * MXU_WIDTH, (half + 1) * MXU_WIDTH)
            yh = y[:, cols]
            inv = lax.rsqrt(_segment_sums(yh * yh, seg_ref) * (1.0 / HEAD_DIM) + EPS)
            mix_scr[rows, ATTN_WIDTH + half * MXU_WIDTH:ATTN_WIDTH + (half + 1) * MXU_WIDTH] = (
                yh * inv * cgain_ref[:, cols]).astype(BF16)
        o_ref[rows, :] = h_ref[rows, :] + jnp.dot(mix_scr[rows, :], wo_ref[...], preferred_element_type=F32)


def _out_proj(ya, cb, ucv, h, conv_w, conv_b, conv_gain, w_out, seg, lp, tm):
    rows = h.shape[0]
    halo_per_tile = tm // SUBLANES
    last_halo = rows // SUBLANES - 1
    return pl.pallas_call(
        functools.partial(_out_body, tm=tm, lp=lp),
        grid=(rows // tm,),
        in_specs=[
            pl.BlockSpec((N_KV_HEADS, tm, GROUP_LANES), lambda i: (0, i, 0)),
            pl.BlockSpec((tm, CONV_WIDTH), lambda i: (i, 0)),
            pl.BlockSpec((tm, CONV_WIDTH), lambda i: (i, 0)),
            pl.BlockSpec((SUBLANES, CONV_WIDTH), lambda i: (jnp.maximum(i * halo_per_tile - 1, 0), 0)),
            pl.BlockSpec((SUBLANES, CONV_WIDTH), lambda i: (jnp.minimum((i + 1) * halo_per_tile, last_halo), 0)),
            pl.BlockSpec((tm, D_MODEL), lambda i: (i, 0)),
            _const_spec((3, CONV_WIDTH)),
            _const_spec((1, CONV_WIDTH)),
            _const_spec((1, CONV_WIDTH)),
            _const_spec((D_MODEL, D_MODEL)),
            _const_spec((2 * MXU_WIDTH, MXU_WIDTH)),
        ],
        out_specs=pl.BlockSpec((tm, D_MODEL), lambda i: (i, 0)),
        out_shape=jax.ShapeDtypeStruct((rows, D_MODEL), F32),
        scratch_shapes=[pltpu.VMEM((tm + 2 * SUBLANES, CONV_WIDTH), F32), pltpu.VMEM((tm, D_MODEL), BF16)],
        compiler_params=_params(1),
        name="out_proj",
    )(ya, cb, ucv, ucv, ucv, h, conv_w, conv_b, conv_gain, w_out, seg)


def _final_body(h_ref, g_ref, o_ref):
    o_ref[0] = _rms(h_ref[...]) * g_ref[...]


def _final_norm(h, gain, batch, n, tm):
    lp = SEQ_PAD + n
    tiles = n // tm
    return pl.pallas_call(
        _final_body,
        grid=(batch, tiles),
        in_specs=[
            pl.BlockSpec((pl.Element(tm), pl.Element(D_MODEL)), lambda b, i: (pl.multiple_of(b * lp + SEQ_PAD + i * tm, LANES), 0)),
            _const_spec((1, D_MODEL)),
        ],
        out_specs=pl.BlockSpec((1, tm, D_MODEL), lambda b, i: (b, i, 0)),
        out_shape=jax.ShapeDtypeStruct((batch, n, D_MODEL), F32),
        compiler_params=_params(2),
        name="final_norm",
    )(h, gain)


def _largest_tile(extent, cap):
    units = extent // LANES
    best = max(d for d in range(1, units + 1) if units % d == 0 and d * LANES <= cap)
    return best * LANES


def _in_column_order():
    halves_heads_pairs = [(half, r, i) for half in range(2) for r in range(KV_GROUP) for i in range(PAIRS)]
    cols = []
    for g in range(N_KV_HEADS):
        cols += [(KV_GROUP * g + r) * HEAD_DIM + 2 * i + half for half, r, i in halves_heads_pairs]
    for g in range(N_KV_HEADS):
        cols += [ATTN_WIDTH + g * HEAD_DIM + 2 * i + half for half, r, i in halves_heads_pairs]
    cols += list(range(ATTN_WIDTH + KV_WIDTH, IN_WIDTH))
    return np.asarray(cols, np.int32)


def _head_gain_order():
    return np.asarray([2 * i + half for half in range(2) for r in range(KV_GROUP) for i in range(PAIRS)], np.int32)


def _rope_tables(n):
    rows = n // GRID_W
    row = jnp.repeat(jnp.arange(rows, dtype=F32), GRID_W)
    col = jnp.tile(jnp.arange(GRID_W, dtype=F32), rows)
    row = jnp.concatenate([jnp.zeros((SEQ_PAD,), F32), row])
    col = jnp.concatenate([jnp.zeros((SEQ_PAD,), F32), col])
    freqs = ROPE_THETA ** (-jnp.arange(ROPE_PAIRS_AXIS, dtype=F32) / ROPE_PAIRS_AXIS)
    ang = jnp.concatenate([row[:, None] * freqs, col[:, None] * freqs], axis=-1)
    return jnp.tile(jnp.cos(ang), (1, KV_GROUP)), jnp.tile(jnp.sin(ang), (1, KV_GROUP))


def _stacked_block_diag_ones(size, block):
    idx = np.arange(size) // block
    seg = idx[:, None] == idx[None, :]
    return jnp.asarray(np.concatenate([seg, seg], axis=0), BF16)


def _trunk(x, meta_tokens, p):
    batch, n, d = x.shape
    assert d == D_MODEL and n % (2 * KV_CHUNK) == 0 and n % GRID_W == 0
    lp = SEQ_PAD + n
    rows = batch * lp
    tm_ffn = _largest_tile(rows, 512)
    tm_seq = _largest_tile(lp, 768)
    tq = tm_seq
    h = jnp.concatenate([
        jnp.zeros((batch, SEQ_PAD - N_META, d), F32),
        jnp.broadcast_to(meta_tokens[None].astype(F32), (batch, N_META, d)),
        x.astype(F32)], axis=1).reshape(rows, d)
    cos, sin = _rope_tables(n)
    for l in range(p["depth"]):
        h = _ffn(h, p["ffn1_norm"][l], p["ffn1_w_gate"][l], p["ffn1_w_up"][l], p["ffn1_w_down"][l], tm_ffn)
        q, k, vt, cb, ucv = _in_proj(h, p["mix_norm"][l], p["w_in"][l], p["qk_gain"][l], cos, sin,
                                     p["seg_pairs"], lp, tm_seq)
        ya = _attention(q, k, vt, p["attn_gain"][l], batch, lp, tq)
        h = _out_proj(ya, cb, ucv, h, p["conv_w"][l], p["conv_b"][l], p["conv_gain"][l], p["w_out"][l],
                      p["seg_heads"], lp, tm_seq)
        h = _ffn(h, p["ffn2_norm"][l], p["ffn2_w_gate"][l], p["ffn2_w_up"][l], p["ffn2_w_down"][l], tm_ffn)
    return _final_norm(h, p["final_norm"], batch, n, 2 * KV_CHUNK)


def kernel(x_prompt, x_sample, meta_tokens, ffn1_norm, ffn1_w_gate, ffn1_w_up, ffn1_w_down, mix_norm, w_in, q_norm, k_norm, conv_w, conv_b, attn_out_norm, conv_out_norm, w_out, ffn2_norm, ffn2_w_gate, ffn2_w_up, ffn2_w_down, final_norm):
    depth = w_in.shape[0]
    order = _head_gain_order()
    q_gain = q_norm[:, order] * (HEAD_DIM ** -0.5 * LOG2_E)
    k_gain = k_norm[:, order]
    p = {
        "depth": depth,
        "ffn1_norm": ffn1_norm[:, None, :],
        "ffn1_w_gate": ffn1_w_gate.astype(BF16),
        "ffn1_w_up": ffn1_w_up.astype(BF16),
        "ffn1_w_down": ffn1_w_down.astype(BF16),
        "mix_norm": mix_norm[:, None, :],
        "w_in": w_in[:, :, _in_column_order()].astype(BF16),
        "qk_gain": jnp.stack([q_gain, q_gain, k_gain, k_gain], axis=1),
        "attn_gain": jnp.pad(attn_out_norm.reshape(depth, N_KV_HEADS, KV_GROUP, HEAD_DIM),
                             ((0, 0), (0, 0), (0, 0), (0, LANES - HEAD_DIM))).reshape(depth, N_KV_HEADS, KV_GROUP, LANES),
        "conv_w": conv_w,
        "conv_b": conv_b[:, None, :],
        "conv_gain": conv_out_norm[:, None, :],
        "w_out": w_out.astype(BF16),
        "ffn2_norm": ffn2_norm[:, None, :],
        "ffn2_w_gate": ffn2_w_gate.astype(BF16),
        "ffn2_w_up": ffn2_w_up.astype(BF16),
        "ffn2_w_down": ffn2_w_down.astype(BF16),
        "final_norm": final_norm[None, :],
        "seg_pairs": _stacked_block_diag_ones(HALF_LANES, PAIRS),
        "seg_heads": _stacked_block_diag_ones(MXU_WIDTH, HEAD_DIM),
    }
    return (_trunk(x_prompt, meta_tokens, p), _trunk(x_sample, meta_tokens, p))
```

```python
import functools

import numpy as np
import jax
import jax.numpy as jnp
from jax import lax
from jax.experimental import pallas as pl
from jax.experimental.pallas import tpu as pltpu

F32 = jnp.float32
BF16 = jnp.bfloat16

D_MODEL = 1024
N_META = 16
GRID_W = 64
HEAD_DIM = 64
N_Q_HEADS = 8
N_KV_HEADS = 2
KV_GROUP = N_Q_HEADS // N_KV_HEADS
ATTN_WIDTH = N_Q_HEADS * HEAD_DIM
KV_WIDTH = N_KV_HEADS * HEAD_DIM
CONV_WIDTH = D_MODEL - ATTN_WIDTH
D_FF = 2816
ROPE_THETA = 10000.0
ROPE_PAIRS_AXIS = HEAD_DIM // 4
EPS = 1e-6
IN_WIDTH = ATTN_WIDTH + 2 * KV_WIDTH + 3 * CONV_WIDTH

LANES = 128
SUBLANES = 8
MXU_WIDTH = 256
SEQ_PAD = LANES
GROUP_LANES = KV_GROUP * HEAD_DIM
HALF_LANES = GROUP_LANES // 2
PAIRS = HEAD_DIM // 2
QK_COLS = 2 * N_KV_HEADS * GROUP_LANES
IN_COLS = QK_COLS + KV_WIDTH + 3 * CONV_WIDTH
ONES_ROWS = 16
VT_ROWS = HEAD_DIM + ONES_ROWS
KV_CHUNK = 4 * LANES
LOOP_CHUNKS = 4
COL_BLOCK = MXU_WIDTH
OUT_ROW_BLOCKS = 4
FFN_ROW_BLOCKS = 2
FF_CHUNKS = ((0, 768), (768, 768), (1536, 768), (2304, 512))
MASKED_SCORE = -1e30
LOG2_E = 1.4426950408889634
VMEM_LIMIT = 56 * 1024 * 1024


def _const_spec(shape):
    zeros = (0,) * len(shape)
    return pl.BlockSpec(shape, lambda *_: zeros, pipeline_mode=pl.Buffered(1))


def _params(n_axes):
    return pltpu.CompilerParams(dimension_semantics=("arbitrary",) * n_axes, vmem_limit_bytes=VMEM_LIMIT)


def _rms(x):
    return x * lax.rsqrt(jnp.mean(x * x, axis=-1, keepdims=True) + EPS)


def _segment_sums(x, seg_ref):
    hi = x.astype(BF16)
    lo = (x - hi.astype(F32)).astype(BF16)
    return jnp.dot(jnp.concatenate([hi, lo], axis=1), seg_ref[...], preferred_element_type=F32)


def _ffn_body(h_ref, g_ref, wg_ref, wu_ref, wd_ref, o_ref, act_scr, *, tm):
    rb = tm // FFN_ROW_BLOCKS
    for blk in range(FFN_ROW_BLOCKS):
        rows = slice(blk * rb, (blk + 1) * rb)
        x = h_ref[rows, :]
        xn = (_rms(x) * g_ref[...]).astype(BF16)
        for start, size in FF_CHUNKS:
            gate = jnp.dot(xn, wg_ref[:, start:start + size], preferred_element_type=F32)
            up = jnp.dot(xn, wu_ref[:, start:start + size], preferred_element_type=F32)
            silu = gate * (1.0 / (1.0 + jnp.exp(-gate)))
            act_scr[rows, start:start + size] = (silu * up).astype(BF16)
        o_ref[rows, :] = x + 0.5 * jnp.dot(act_scr[rows, :], wd_ref[...], preferred_element_type=F32)


def _ffn(h, gain, w_gate, w_up, w_down, tm):
    rows = h.shape[0]
    return pl.pallas_call(
        functools.partial(_ffn_body, tm=tm),
        grid=(rows // tm,),
        in_specs=[
            pl.BlockSpec((tm, D_MODEL), lambda i: (i, 0)),
            _const_spec((1, D_MODEL)),
            _const_spec((D_MODEL, D_FF)),
            _const_spec((D_MODEL, D_FF)),
            _const_spec((D_FF, D_MODEL)),
        ],
        out_specs=pl.BlockSpec((tm, D_MODEL), lambda i: (i, 0)),
        out_shape=jax.ShapeDtypeStruct((rows, D_MODEL), F32),
        scratch_shapes=[pltpu.VMEM((tm, D_FF), BF16)],
        compiler_params=_params(1),
        name="ffn",
    )(h, gain, w_gate, w_up, w_down)


def _in_body(h_ref, g_ref, w_ref, qkg_ref, cos_ref, sin_ref, seg_ref,
             q_ref, k_ref, vt_ref, cb_ref, ucv_ref, *, tm):
    xn = (_rms(h_ref[...]) * g_ref[...]).astype(BF16)
    u_all = jnp.dot(xn, w_ref[...], preferred_element_type=F32)
    cos = cos_ref[...]
    sin = sin_ref[...]
    for c in range(2 * N_KV_HEADS):
        x0 = u_all[:, c * GROUP_LANES:c * GROUP_LANES + HALF_LANES]
        x1 = u_all[:, c * GROUP_LANES + HALF_LANES:(c + 1) * GROUP_LANES]
        ssq = _segment_sums(x0 * x0 + x1 * x1, seg_ref)
        inv = lax.rsqrt(ssq * (1.0 / HEAD_DIM) + EPS)
        a0 = x0 * inv * qkg_ref[c:c + 1, :HALF_LANES]
        a1 = x1 * inv * qkg_ref[c:c + 1, HALF_LANES:]
        o0 = a0 * cos - a1 * sin
        o1 = a0 * sin + a1 * cos
        out_ref = q_ref if c < N_KV_HEADS else k_ref
        out_ref[c % N_KV_HEADS, :, :HALF_LANES] = o0.astype(BF16)
        out_ref[c % N_KV_HEADS, :, HALF_LANES:] = o1.astype(BF16)
    v = u_all[:, QK_COLS:QK_COLS + KV_WIDTH]
    lane = lax.broadcasted_iota(jnp.int32, v.shape, 1)
    for g in range(N_KV_HEADS):
        v_ones = jnp.where(lane < HEAD_DIM, v if g == 0 else pltpu.roll(v, HEAD_DIM, 1), 1.0)
        for j in range(tm // LANES):
            vt_ref[g, j] = v_ones[j * LANES:(j + 1) * LANES, :].T[:VT_ROWS, :].astype(BF16)
    conv0 = QK_COLS + KV_WIDTH
    cb_ref[...] = u_all[:, conv0:conv0 + CONV_WIDTH]
    ucv_ref[...] = (u_all[:, conv0 + CONV_WIDTH:conv0 + 2 * CONV_WIDTH]
                    * u_all[:, conv0 + 2 * CONV_WIDTH:conv0 + 3 * CONV_WIDTH])


def _in_proj(h, gain, w_in, qk_gain, cos, sin, seg, lp, tm):
    rows = h.shape[0]
    tiles_per_seq = lp // tm
    return pl.pallas_call(
        functools.partial(_in_body, tm=tm),
        grid=(rows // tm,),
        in_specs=[
            pl.BlockSpec((tm, D_MODEL), lambda i: (i, 0)),
            _const_spec((1, D_MODEL)),
            _const_spec((D_MODEL, IN_COLS)),
            _const_spec((2 * N_KV_HEADS, GROUP_LANES)),
            pl.BlockSpec((tm, HALF_LANES), lambda i: (i % tiles_per_seq, 0)),
            pl.BlockSpec((tm, HALF_LANES), lambda i: (i % tiles_per_seq, 0)),
            _const_spec((2 * HALF_LANES, HALF_LANES)),
        ],
        out_specs=[
            pl.BlockSpec((N_KV_HEADS, tm, GROUP_LANES), lambda i: (0, i, 0)),
            pl.BlockSpec((N_KV_HEADS, tm, GROUP_LANES), lambda i: (0, i, 0)),
            pl.BlockSpec((N_KV_HEADS, tm // LANES, VT_ROWS, LANES), lambda i: (0, i, 0, 0)),
            pl.BlockSpec((tm, CONV_WIDTH), lambda i: (i, 0)),
            pl.BlockSpec((tm, CONV_WIDTH), lambda i: (i, 0)),
        ],
        out_shape=[
            jax.ShapeDtypeStruct((N_KV_HEADS, rows, GROUP_LANES), BF16),
            jax.ShapeDtypeStruct((N_KV_HEADS, rows, GROUP_LANES), BF16),
            jax.ShapeDtypeStruct((N_KV_HEADS, rows // LANES, VT_ROWS, LANES), BF16),
            jax.ShapeDtypeStruct((rows, CONV_WIDTH), F32),
            jax.ShapeDtypeStruct((rows, CONV_WIDTH), F32),
        ],
        compiler_params=_params(1),
        name="in_proj",
    )(h, gain, w_in, qk_gain, cos, sin, seg)


def _attn_body(q_ref, k_ref, vt_ref, gain_ref, o_ref, qt_scr, s_scr, mx_scr, m_scr, acc_scr, *, tq, n_steps):
    qt = q_ref[0].astype(F32).T
    dim = lax.broadcasted_iota(jnp.int32, qt.shape, 0)
    head_of_dim = (dim // PAIRS) % KV_GROUP
    for r in range(KV_GROUP):
        qt_scr[:, r * tq:(r + 1) * tq] = jnp.where(head_of_dim == r, qt, 0.0).astype(BF16)

    def sublane_tiles(x):
        return [x[j * SUBLANES:(j + 1) * SUBLANES, :] for j in range(x.shape[0] // SUBLANES)]

    col_blocks = [slice(b * COL_BLOCK, (b + 1) * COL_BLOCK) for b in range(KV_GROUP * tq // COL_BLOCK)]

    def produce(c, slot, cols):
        k = k_ref[0, pl.ds(pl.multiple_of(SEQ_PAD + c * KV_CHUNK, LANES), KV_CHUNK), :]
        s = jnp.dot(k, qt_scr[:, cols], preferred_element_type=F32)
        s_scr[slot, :, cols] = s
        mx_scr[slot, :, cols] = functools.reduce(jnp.maximum, sublane_tiles(s))

    def accumulate(s, m_cur, vt, cols):
        m_old = m_scr[:, cols]
        m_new = jnp.maximum(m_old, m_cur)
        p = jnp.exp2(s - m_new).astype(BF16)
        pv = jnp.dot(vt, p, preferred_element_type=F32)
        acc_scr[:, cols] = jnp.exp2(m_old - m_new) * acc_scr[:, cols] + pv
        m_scr[:, cols] = m_new

    def consume(c, slot, cols):
        vt4 = vt_ref[0, pl.ds(1 + c * (KV_CHUNK // LANES), KV_CHUNK // LANES)]
        vt = jnp.concatenate([vt4[j] for j in range(KV_CHUNK // LANES)], axis=1)
        accumulate(s_scr[slot, :, cols], jnp.max(mx_scr[slot, :, cols], axis=0, keepdims=True), vt, cols)

    def stage(produced, consumed):
        for cols in col_blocks:
            if produced is not None:
                produce(produced[0], produced[1], cols)
            if consumed is not None:
                consume(consumed[0], consumed[1], cols)

    m_scr[...] = jnp.full(m_scr.shape, MASKED_SCORE, F32)
    acc_scr[...] = jnp.zeros(acc_scr.shape, F32)

    stage((0, 0), None)

    per_trip = LOOP_CHUNKS if n_steps > 2 * LOOP_CHUNKS else 2

    def trip(i, carry):
        for u in range(per_trip):
            stage((i * per_trip + u + 1, (u + 1) % 2), (i * per_trip + u, u % 2))
        return carry

    looped = (n_steps - 2) // per_trip * per_trip
    lax.fori_loop(0, looped // per_trip, trip, 0)
    for c in range(looped, n_steps - 1):
        stage((c + 1, (c + 1) % 2), (c, c % 2))
    s = jnp.dot(k_ref[0, 0:SEQ_PAD, :], qt_scr[...], preferred_element_type=F32)
    key = lax.broadcasted_iota(jnp.int32, s.shape, 0)
    s = jnp.where(key >= SEQ_PAD - N_META, s, MASKED_SCORE)
    stage(None, (n_steps - 1, 1))
    for cols in col_blocks:
        accumulate(s[:, cols], jnp.max(s[:, cols], axis=0, keepdims=True), vt_ref[0, 0], cols)

    acc = acc_scr[...]
    is_value = lax.broadcasted_iota(jnp.int32, acc.shape, 0) < HEAD_DIM
    weight = jnp.where(is_value, 1.0 / HEAD_DIM, EPS / ONES_ROWS)
    yt = acc * lax.rsqrt(jnp.sum(acc * acc * weight, axis=0, keepdims=True))
    yt = jnp.concatenate([yt, jnp.zeros((LANES - VT_ROWS, yt.shape[1]), F32)], axis=0)
    heads = [yt[:, r * tq:(r + 1) * tq].T * gain_ref[0, r:r + 1, :] for r in range(KV_GROUP)]
    is_value = lax.broadcasted_iota(jnp.int32, (tq, LANES), 1) < HEAD_DIM
    o_ref[0, :, :LANES] = jnp.where(is_value, heads[0], pltpu.roll(heads[1], HEAD_DIM, 1)).astype(BF16)
    o_ref[0, :, LANES:] = jnp.where(is_value, heads[2], pltpu.roll(heads[3], HEAD_DIM, 1)).astype(BF16)


def _attention(q, k, vt, gain, batch, lp, tq):
    rows = batch * lp
    q_tiles = lp // tq
    chunks = lp // LANES
    n_steps = (lp - SEQ_PAD) // KV_CHUNK
    return pl.pallas_call(
        functools.partial(_attn_body, tq=tq, n_steps=n_steps),
        grid=(batch, N_KV_HEADS, q_tiles),
        in_specs=[
            pl.BlockSpec((1, tq, GROUP_LANES), lambda b, g, i: (g, b * q_tiles + i, 0)),
            pl.BlockSpec((1, lp, GROUP_LANES), lambda b, g, i: (g, b, 0)),
            pl.BlockSpec((1, chunks, VT_ROWS, LANES), lambda b, g, i: (g, b, 0, 0)),
            pl.BlockSpec((1, KV_GROUP, LANES), lambda b, g, i: (g, 0, 0)),
        ],
        out_specs=pl.BlockSpec((1, tq, GROUP_LANES), lambda b, g, i: (g, b * q_tiles + i, 0)),
        out_shape=jax.ShapeDtypeStruct((N_KV_HEADS, rows, GROUP_LANES), BF16),
        scratch_shapes=[
            pltpu.VMEM((GROUP_LANES, KV_GROUP * tq), BF16),
            pltpu.VMEM((2, KV_CHUNK, KV_GROUP * tq + LANES), F32),
            pltpu.VMEM((2, SUBLANES, KV_GROUP * tq), F32),
            pltpu.VMEM((1, KV_GROUP * tq), F32),
            pltpu.VMEM((VT_ROWS, KV_GROUP * tq), F32),
        ],
        compiler_params=_params(3),
        name="attn",
    )(q, k, vt, gain)


def _out_body(ya_ref, cb_ref, ucv_ref, prev_ref, next_ref, h_ref, cw_ref, cbias_ref, cgain_ref,
              wo_ref, seg_ref, o_ref, ext_scr, mix_scr, *, tm, lp):
    ext_scr[0:SUBLANES, :] = prev_ref[...]
    ext_scr[SUBLANES:SUBLANES + tm, :] = ucv_ref[...]
    ext_scr[SUBLANES + tm:, :] = next_ref[...]
    tile_pos = (pl.program_id(0) % (lp // tm)) * tm
    rb = tm // OUT_ROW_BLOCKS
    for blk in range(OUT_ROW_BLOCKS):
        r0 = blk * rb
        rows = slice(r0, r0 + rb)
        pos = tile_pos + r0 + lax.broadcasted_iota(jnp.int32, (rb, 1), 0)
        left = jnp.where(pos == SEQ_PAD - N_META, 0.0, ext_scr[SUBLANES - 1 + r0:SUBLANES - 1 + r0 + rb, :])
        right = jnp.where(pos == lp - 1, 0.0, ext_scr[SUBLANES + 1 + r0:SUBLANES + 1 + r0 + rb, :])
        conv = (left * cw_ref[0:1, :] + ucv_ref[rows, :] * cw_ref[1:2, :] + right * cw_ref[2:3, :]
                + cbias_ref[...])
        y = cb_ref[rows, :] * conv
        for g in range(N_KV_HEADS):
            mix_scr[rows, g * GROUP_LANES:(g + 1) * GROUP_LANES] = ya_ref[g, rows, :]
        for half in range(CONV_WIDTH // MXU_WIDTH):
            cols = slice(half * MXU_WIDTH, (half + 1) * MXU_WIDTH)
            yh = y[:, cols]
            inv = lax.rsqrt(_segment_sums(yh * yh, seg_ref) * (1.0 / HEAD_DIM) + EPS)
            mix_scr[rows, ATTN_WIDTH + half * MXU_WIDTH:ATTN_WIDTH + (half + 1) * MXU_WIDTH] = (
                yh * inv * cgain_ref[:, cols]).astype(BF16)
        o_ref[rows, :] = h_ref[rows, :] + jnp.dot(mix_scr[rows, :], wo_ref[...], preferred_element_type=F32)


def _out_proj(ya, cb, ucv, h, conv_w, conv_b, conv_gain, w_out, seg, lp, tm):
    rows = h.shape[0]
    halo_per_tile = tm // SUBLANES
    last_halo = rows // SUBLANES - 1
    return pl.pallas_call(
        functools.partial(_out_body, tm=tm, lp=lp),
        grid=(rows // tm,),
        in_specs=[
            pl.BlockSpec((N_KV_HEADS, tm, GROUP_LANES), lambda i: (0, i, 0)),
            pl.BlockSpec((tm, CONV_WIDTH), lambda i: (i, 0)),
            pl.BlockSpec((tm, CONV_WIDTH), lambda i: (i, 0)),
            pl.BlockSpec((SUBLANES, CONV_WIDTH), lambda i: (jnp.maximum(i * halo_per_tile - 1, 0), 0)),
            pl.BlockSpec((SUBLANES, CONV_WIDTH), lambda i: (jnp.minimum((i + 1) * halo_per_tile, last_halo), 0)),
            pl.BlockSpec((tm, D_MODEL), lambda i: (i, 0)),
            _const_spec((3, CONV_WIDTH)),
            _const_spec((1, CONV_WIDTH)),
            _const_spec((1, CONV_WIDTH)),
            _const_spec((D_MODEL, D_MODEL)),
            _const_spec((2 * MXU_WIDTH, MXU_WIDTH)),
        ],
        out_specs=pl.BlockSpec((tm, D_MODEL), lambda i: (i, 0)),
        out_shape=jax.ShapeDtypeStruct((rows, D_MODEL), F32),
        scratch_shapes=[pltpu.VMEM((tm + 2 * SUBLANES, CONV_WIDTH), F32), pltpu.VMEM((tm, D_MODEL), BF16)],
        compiler_params=_params(1),
        name="out_proj",
    )(ya, cb, ucv, ucv, ucv, h, conv_w, conv_b, conv_gain, w_out, seg)


def _final_body(h_ref, g_ref, o_ref):
    o_ref[0] = _rms(h_ref[...]) * g_ref[...]


def _final_norm(h, gain, batch, n, tm):
    lp = SEQ_PAD + n
    tiles = n // tm
    return pl.pallas_call(
        _final_body,
        grid=(batch, tiles),
        in_specs=[
            pl.BlockSpec((pl.Element(tm), pl.Element(D_MODEL)), lambda b, i: (pl.multiple_of(b * lp + SEQ_PAD + i * tm, LANES), 0)),
            _const_spec((1, D_MODEL)),
        ],
        out_specs=pl.BlockSpec((1, tm, D_MODEL), lambda b, i: (b, i, 0)),
        out_shape=jax.ShapeDtypeStruct((batch, n, D_MODEL), F32),
        compiler_params=_params(2),
        name="final_norm",
    )(h, gain)


def _largest_tile(extent, cap):
    units = extent // LANES
    best = max(d for d in range(1, units + 1) if units % d == 0 and d * LANES <= cap)
    return best * LANES


def _in_column_order():
    halves_heads_pairs = [(half, r, i) for half in range(2) for r in range(KV_GROUP) for i in range(PAIRS)]
    cols = []
    for g in range(N_KV_HEADS):
        cols += [(KV_GROUP * g + r) * HEAD_DIM + 2 * i + half for half, r, i in halves_heads_pairs]
    for g in range(N_KV_HEADS):
        cols += [ATTN_WIDTH + g * HEAD_DIM + 2 * i + half for half, r, i in halves_heads_pairs]
    cols += list(range(ATTN_WIDTH + KV_WIDTH, IN_WIDTH))
    return np.asarray(cols, np.int32)


def _head_gain_order():
    return np.asarray([2 * i + half for half in range(2) for r in range(KV_GROUP) for i in range(PAIRS)], np.int32)


def _rope_tables(n):
    rows = n // GRID_W
    row = jnp.repeat(jnp.arange(rows, dtype=F32), GRID_W)
    col = jnp.tile(jnp.arange(GRID_W, dtype=F32), rows)
    row = jnp.concatenate([jnp.zeros((SEQ_PAD,), F32), row])
    col = jnp.concatenate([jnp.zeros((SEQ_PAD,), F32), col])
    freqs = ROPE_THETA ** (-jnp.arange(ROPE_PAIRS_AXIS, dtype=F32) / ROPE_PAIRS_AXIS)
    ang = jnp.concatenate([row[:, None] * freqs, col[:, None] * freqs], axis=-1)
    return jnp.tile(jnp.cos(ang), (1, KV_GROUP)), jnp.tile(jnp.sin(ang), (1, KV_GROUP))


def _stacked_block_diag_ones(size, block):
    idx = np.arange(size) // block
    seg = idx[:, None] == idx[None, :]
    return jnp.asarray(np.concatenate([seg, seg], axis=0), BF16)


def _trunk(x, meta_tokens, p):
    batch, n, d = x.shape
    assert d == D_MODEL and n % (2 * KV_CHUNK) == 0 and n % GRID_W == 0
    lp = SEQ_PAD + n
    rows = batch * lp
    tm_ffn = _largest_tile(rows, 1024)
    tm_seq = _largest_tile(lp, 768)
    tq = tm_seq
    h = jnp.concatenate([
        jnp.zeros((batch, SEQ_PAD - N_META, d), F32),
        jnp.broadcast_to(meta_tokens[None].astype(F32), (batch, N_META, d)),
        x.astype(F32)], axis=1).reshape(rows, d)
    cos, sin = _rope_tables(n)
    for l in range(p["depth"]):
        h = _ffn(h, p["ffn1_norm"][l], p["ffn1_w_gate"][l], p["ffn1_w_up"][l], p["ffn1_w_down"][l], tm_ffn)
        q, k, vt, cb, ucv = _in_proj(h, p["mix_norm"][l], p["w_in"][l], p["qk_gain"][l], cos, sin,
                                     p["seg_pairs"], lp, tm_seq)
        ya = _attention(q, k, vt, p["attn_gain"][l], batch, lp, tq)
        h = _out_proj(ya, cb, ucv, h, p["conv_w"][l], p["conv_b"][l], p["conv_gain"][l], p["w_out"][l],
                      p["seg_heads"], lp, tm_seq)
        h = _ffn(h, p["ffn2_norm"][l], p["ffn2_w_gate"][l], p["ffn2_w_up"][l], p["ffn2_w_down"][l], tm_ffn)
    return _final_norm(h, p["final_norm"], batch, n, 2 * KV_CHUNK)


def kernel(x_prompt, x_sample, meta_tokens, ffn1_norm, ffn1_w_gate, ffn1_w_up, ffn1_w_down, mix_norm, w_in, q_norm, k_norm, conv_w, conv_b, attn_out_norm, conv_out_norm, w_out, ffn2_norm, ffn2_w_gate, ffn2_w_up, ffn2_w_down, final_norm):
    depth = w_in.shape[0]
    order = _head_gain_order()
    q_gain = q_norm[:, order] * (HEAD_DIM ** -0.5 * LOG2_E)
    k_gain = k_norm[:, order]
    p = {
        "depth": depth,
        "ffn1_norm": ffn1_norm[:, None, :],
        "ffn1_w_gate": ffn1_w_gate.astype(BF16),
        "ffn1_w_up": ffn1_w_up.astype(BF16),
        "ffn1_w_down": ffn1_w_down.astype(BF16),
        "mix_norm": mix_norm[:, None, :],
        "w_in": w_in[:, :, _in_column_order()].astype(BF16),
        "qk_gain": jnp.stack([q_gain, q_gain, k_gain, k_gain], axis=1),
        "attn_gain": jnp.pad(attn_out_norm.reshape(depth, N_KV_HEADS, KV_GROUP, HEAD_DIM),
                             ((0, 0), (0, 0), (0, 0), (0, LANES - HEAD_DIM))).reshape(depth, N_KV_HEADS, KV_GROUP, LANES),
        "conv_w": conv_w,
        "conv_b": conv_b[:, None, :],
        "conv_gain": conv_out_norm[:, None, :],
        "w_out": w_out.astype(BF16),
        "ffn2_norm": ffn2_norm[:, None, :],
        "ffn2_w_gate": ffn2_w_gate.astype(BF16),
        "ffn2_w_up": ffn2_w_up.astype(BF16),
        "ffn2_w_down": ffn2_w_down.astype(BF16),
        "final_norm": final_norm[None, :],
        "seg_pairs": _stacked_block_diag_ones(HALF_LANES, PAIRS),
        "seg_heads": _stacked_block_diag_ones(MXU_WIDTH, HEAD_DIM),
    }
    return (_trunk(x_prompt, meta_tokens, p), _trunk(x_sample, meta_tokens, p))
```

```python
import functools

import numpy as np
import jax
import jax.numpy as jnp
from jax import lax
from jax.experimental import pallas as pl
from jax.experimental.pallas import tpu as pltpu

F32 = jnp.float32
BF16 = jnp.bfloat16

D_MODEL = 1024
N_META = 16
GRID_W = 64
HEAD_DIM = 64
N_Q_HEADS = 8
N_KV_HEADS = 2
KV_GROUP = N_Q_HEADS // N_KV_HEADS
ATTN_WIDTH = N_Q_HEADS * HEAD_DIM
KV_WIDTH = N_KV_HEADS * HEAD_DIM
CONV_WIDTH = D_MODEL - ATTN_WIDTH
D_FF = 2816
ROPE_THETA = 10000.0
ROPE_PAIRS_AXIS = HEAD_DIM // 4
EPS = 1e-6
IN_WIDTH = ATTN_WIDTH + 2 * KV_WIDTH + 3 * CONV_WIDTH

LANES = 128
SUBLANES = 8
MXU_WIDTH = 256
SEQ_PAD = LANES
GROUP_LANES = KV_GROUP * HEAD_DIM
HALF_LANES = GROUP_LANES // 2
PAIRS = HEAD_DIM // 2
QK_COLS = 2 * N_KV_HEADS * GROUP_LANES
IN_COLS = QK_COLS + KV_WIDTH + 3 * CONV_WIDTH
ONES_ROWS = 16
VT_ROWS = HEAD_DIM + ONES_ROWS
KV_CHUNK = 4 * LANES
LOOP_CHUNKS = 6
COL_BLOCK = MXU_WIDTH
OUT_ROW_BLOCKS = 4
FFN_ROW_BLOCKS = 4
FF_CHUNKS = ((0, 768), (768, 768), (1536, 768), (2304, 512))
MASKED_SCORE = -1e30
LOG2_E = 1.4426950408889634
VMEM_LIMIT = 56 * 1024 * 1024


def _const_spec(shape):
    zeros = (0,) * len(shape)
    return pl.BlockSpec(shape, lambda *_: zeros, pipeline_mode=pl.Buffered(1))


def _params(n_axes):
    return pltpu.CompilerParams(dimension_semantics=("arbitrary",) * n_axes, vmem_limit_bytes=VMEM_LIMIT)


def _rms(x):
    return x * lax.rsqrt(jnp.mean(x * x, axis=-1, keepdims=True) + EPS)


def _segment_sums(x, seg_ref):
    hi = x.astype(BF16)
    lo = (x - hi.astype(F32)).astype(BF16)
    return jnp.dot(jnp.concatenate([hi, lo], axis=1), seg_ref[...], preferred_element_type=F32)


def _ffn_body(h_ref, g_ref, wg_ref, wu_ref, wd_ref, o_ref, act_scr, *, tm):
    rb = tm // FFN_ROW_BLOCKS
    for blk in range(FFN_ROW_BLOCKS):
        rows = slice(blk * rb, (blk + 1) * rb)
        x = h_ref[rows, :]
        xn = (_rms(x) * g_ref[...]).astype(BF16)
        for start, size in FF_CHUNKS:
            gate = jnp.dot(xn, wg_ref[:, start:start + size], preferred_element_type=F32)
            up = jnp.dot(xn, wu_ref[:, start:start + size], preferred_element_type=F32)
            silu = gate * (1.0 / (1.0 + jnp.exp(-gate)))
            act_scr[rows, start:start + size] = (silu * up).astype(BF16)
        o_ref[rows, :] = x + 0.5 * jnp.dot(act_scr[rows, :], wd_ref[...], preferred_element_type=F32)


def _ffn(h, gain, w_gate, w_up, w_down, tm):
    rows = h.shape[0]
    return pl.pallas_call(
        functools.partial(_ffn_body, tm=tm),
        grid=(rows // tm,),
        in_specs=[
            pl.BlockSpec((tm, D_MODEL), lambda i: (i, 0)),
            _const_spec((1, D_MODEL)),
            _const_spec((D_MODEL, D_FF)),
            _const_spec((D_MODEL, D_FF)),
            _const_spec((D_FF, D_MODEL)),
        ],
        out_specs=pl.BlockSpec((tm, D_MODEL), lambda i: (i, 0)),
        out_shape=jax.ShapeDtypeStruct((rows, D_MODEL), F32),
        scratch_shapes=[pltpu.VMEM((tm, D_FF), BF16)],
        compiler_params=_params(1),
        name="ffn",
    )(h, gain, w_gate, w_up, w_down)


def _in_body(h_ref, g_ref, w_ref, qkg_ref, cos_ref, sin_ref, seg_ref,
             q_ref, k_ref, vt_ref, cb_ref, ucv_ref, *, tm):
    xn = (_rms(h_ref[...]) * g_ref[...]).astype(BF16)
    u_all = jnp.dot(xn, w_ref[...], preferred_element_type=F32)
    cos = cos_ref[...]
    sin = sin_ref[...]
    for c in range(2 * N_KV_HEADS):
        x0 = u_all[:, c * GROUP_LANES:c * GROUP_LANES + HALF_LANES]
        x1 = u_all[:, c * GROUP_LANES + HALF_LANES:(c + 1) * GROUP_LANES]
        ssq = _segment_sums(x0 * x0 + x1 * x1, seg_ref)
        inv = lax.rsqrt(ssq * (1.0 / HEAD_DIM) + EPS)
        a0 = x0 * inv * qkg_ref[c:c + 1, :HALF_LANES]
        a1 = x1 * inv * qkg_ref[c:c + 1, HALF_LANES:]
        o0 = a0 * cos - a1 * sin
        o1 = a0 * sin + a1 * cos
        out_ref = q_ref if c < N_KV_HEADS else k_ref
        out_ref[c % N_KV_HEADS, :, :HALF_LANES] = o0.astype(BF16)
        out_ref[c % N_KV_HEADS, :, HALF_LANES:] = o1.astype(BF16)
    v = u_all[:, QK_COLS:QK_COLS + KV_WIDTH]
    lane = lax.broadcasted_iota(jnp.int32, v.shape, 1)
    for g in range(N_KV_HEADS):
        v_ones = jnp.where(lane < HEAD_DIM, v if g == 0 else pltpu.roll(v, HEAD_DIM, 1), 1.0)
        for j in range(tm // LANES):
            vt_ref[g, j] = v_ones[j * LANES:(j + 1) * LANES, :].T[:VT_ROWS, :].astype(BF16)
    conv0 = QK_COLS + KV_WIDTH
    cb_ref[...] = u_all[:, conv0:conv0 + CONV_WIDTH]
    ucv_ref[...] = (u_all[:, conv0 + CONV_WIDTH:conv0 + 2 * CONV_WIDTH]
                    * u_all[:, conv0 + 2 * CONV_WIDTH:conv0 + 3 * CONV_WIDTH])


def _in_proj(h, gain, w_in, qk_gain, cos, sin, seg, lp, tm):
    rows = h.shape[0]
    tiles_per_seq = lp // tm
    return pl.pallas_call(
        functools.partial(_in_body, tm=tm),
        grid=(rows // tm,),
        in_specs=[
            pl.BlockSpec((tm, D_MODEL), lambda i: (i, 0)),
            _const_spec((1, D_MODEL)),
            _const_spec((D_MODEL, IN_COLS)),
            _const_spec((2 * N_KV_HEADS, GROUP_LANES)),
            pl.BlockSpec((tm, HALF_LANES), lambda i: (i % tiles_per_seq, 0)),
            pl.BlockSpec((tm, HALF_LANES), lambda i: (i % tiles_per_seq, 0)),
            _const_spec((2 * HALF_LANES, HALF_LANES)),
        ],
        out_specs=[
            pl.BlockSpec((N_KV_HEADS, tm, GROUP_LANES), lambda i: (0, i, 0)),
            pl.BlockSpec((N_KV_HEADS, tm, GROUP_LANES), lambda i: (0, i, 0)),
            pl.BlockSpec((N_KV_HEADS, tm // LANES, VT_ROWS, LANES), lambda i: (0, i, 0, 0)),
            pl.BlockSpec((tm, CONV_WIDTH), lambda i: (i, 0)),
            pl.BlockSpec((tm, CONV_WIDTH), lambda i: (i, 0)),
        ],
        out_shape=[
            jax.ShapeDtypeStruct((N_KV_HEADS, rows, GROUP_LANES), BF16),
            jax.ShapeDtypeStruct((N_KV_HEADS, rows, GROUP_LANES), BF16),
            jax.ShapeDtypeStruct((N_KV_HEADS, rows // LANES, VT_ROWS, LANES), BF16),
            jax.ShapeDtypeStruct((rows, CONV_WIDTH), F32),
            jax.ShapeDtypeStruct((rows, CONV_WIDTH), F32),
        ],
        compiler_params=_params(1),
        name="in_proj",
    )(h, gain, w_in, qk_gain, cos, sin, seg)


def _attn_body(q_ref, k_ref, vt_ref, gain_ref, o_ref, qt_scr, s_scr, mx_scr, m_scr, acc_scr, *, tq, n_steps):
    qt = q_ref[0].astype(F32).T
    dim = lax.broadcasted_iota(jnp.int32, qt.shape, 0)
    head_of_dim = (dim // PAIRS) % KV_GROUP
    for r in range(KV_GROUP):
        qt_scr[:, r * tq:(r + 1) * tq] = jnp.where(head_of_dim == r, qt, 0.0).astype(BF16)

    def sublane_tiles(x):
        return [x[j * SUBLANES:(j + 1) * SUBLANES, :] for j in range(x.shape[0] // SUBLANES)]

    col_blocks = [slice(b * COL_BLOCK, (b + 1) * COL_BLOCK) for b in range(KV_GROUP * tq // COL_BLOCK)]

    def produce(c, slot, cols):
        k = k_ref[0, pl.ds(pl.multiple_of(SEQ_PAD + c * KV_CHUNK, LANES), KV_CHUNK), :]
        s = jnp.dot(k, qt_scr[:, cols], preferred_element_type=F32)
        s_scr[slot, :, cols] = s
        mx_scr[slot, :, cols] = functools.reduce(jnp.maximum, sublane_tiles(s))

    def accumulate(s, m_cur, vt, cols):
        m_old = m_scr[:, cols]
        m_new = jnp.maximum(m_old, m_cur)
        p = jnp.exp2(s - m_new).astype(BF16)
        pv = jnp.dot(vt, p, preferred_element_type=F32)
        acc_scr[:, cols] = jnp.exp2(m_old - m_new) * acc_scr[:, cols] + pv
        m_scr[:, cols] = m_new

    def consume(c, slot, cols):
        vt4 = vt_ref[0, pl.ds(1 + c * (KV_CHUNK // LANES), KV_CHUNK // LANES)]
        vt = jnp.concatenate([vt4[j] for j in range(KV_CHUNK // LANES)], axis=1)
        accumulate(s_scr[slot, :, cols], jnp.max(mx_scr[slot, :, cols], axis=0, keepdims=True), vt, cols)

    def stage(produced, consumed):
        for cols in col_blocks:
            if produced is not None:
                produce(produced[0], produced[1], cols)
            if consumed is not None:
                consume(consumed[0], consumed[1], cols)

    m_scr[...] = jnp.full(m_scr.shape, MASKED_SCORE, F32)
    acc_scr[...] = jnp.zeros(acc_scr.shape, F32)

    stage((0, 0), None)

    per_trip = LOOP_CHUNKS if n_steps > 2 * LOOP_CHUNKS else 2

    def trip(i, carry):
        for u in range(per_trip):
            stage((i * per_trip + u + 1, (u + 1) % 2), (i * per_trip + u, u % 2))
        return carry

    looped = (n_steps - 2) // per_trip * per_trip
    lax.fori_loop(0, looped // per_trip, trip, 0)
    for c in range(looped, n_steps - 1):
        stage((c + 1, (c + 1) % 2), (c, c % 2))
    s = jnp.dot(k_ref[0, 0:SEQ_PAD, :], qt_scr[...], preferred_element_type=F32)
    key = lax.broadcasted_iota(jnp.int32, s.shape, 0)
    s = jnp.where(key >= SEQ_PAD - N_META, s, MASKED_SCORE)
    stage(None, (n_steps - 1, 1))
    for cols in col_blocks:
        accumulate(s[:, cols], jnp.max(s[:, cols], axis=0, keepdims=True), vt_ref[0, 0], cols)

    acc = acc_scr[...]
    is_value = lax.broadcasted_iota(jnp.int32, acc.shape, 0) < HEAD_DIM
    weight = jnp.where(is_value, 1.0 / HEAD_DIM, EPS / ONES_ROWS)
    yt = acc * lax.rsqrt(jnp.sum(acc * acc * weight, axis=0, keepdims=True))
    yt = jnp.concatenate([yt, jnp.zeros((LANES - VT_ROWS, yt.shape[1]), F32)], axis=0)
    heads = [yt[:, r * tq:(r + 1) * tq].T * gain_ref[0, r:r + 1, :] for r in range(KV_GROUP)]
    is_value = lax.broadcasted_iota(jnp.int32, (tq, LANES), 1) < HEAD_DIM
    o_ref[0, :, :LANES] = jnp.where(is_value, heads[0], pltpu.roll(heads[1], HEAD_DIM, 1)).astype(BF16)
    o_ref[0, :, LANES:] = jnp.where(is_value, heads[2], pltpu.roll(heads[3], HEAD_DIM, 1)).astype(BF16)


def _attention(q, k, vt, gain, batch, lp, tq):
    rows = batch * lp
    q_tiles = lp // tq
    chunks = lp // LANES
    n_steps = (lp - SEQ_PAD) // KV_CHUNK
    return pl.pallas_call(
        functools.partial(_attn_body, tq=tq, n_steps=n_steps),
        grid=(batch, N_KV_HEADS, q_tiles),
        in_specs=[
            pl.BlockSpec((1, tq, GROUP_LANES), lambda b, g, i: (g, b * q_tiles + i, 0)),
            pl.BlockSpec((1, lp, GROUP_LANES), lambda b, g, i: (g, b, 0)),
            pl.BlockSpec((1, chunks, VT_ROWS, LANES), lambda b, g, i: (g, b, 0, 0)),
            pl.BlockSpec((1, KV_GROUP, LANES), lambda b, g, i: (g, 0, 0)),
        ],
        out_specs=pl.BlockSpec((1, tq, GROUP_LANES), lambda b, g, i: (g, b * q_tiles + i, 0)),
        out_shape=jax.ShapeDtypeStruct((N_KV_HEADS, rows, GROUP_LANES), BF16),
        scratch_shapes=[
            pltpu.VMEM((GROUP_LANES, KV_GROUP * tq), BF16),
            pltpu.VMEM((2, KV_CHUNK, KV_GROUP * tq + LANES), F32),
            pltpu.VMEM((2, SUBLANES, KV_GROUP * tq), F32),
            pltpu.VMEM((1, KV_GROUP * tq), F32),
            pltpu.VMEM((VT_ROWS, KV_GROUP * tq), F32),
        ],
        compiler_params=_params(3),
        name="attn",
    )(q, k, vt, gain)


def _out_body(ya_ref, cb_ref, ucv_ref, prev_ref, next_ref, h_ref, cw_ref, cbias_ref, cgain_ref,
              wo_ref, seg_ref, o_ref, ext_scr, mix_scr, *, tm, lp):
    ext_scr[0:SUBLANES, :] = prev_ref[...]
    ext_scr[SUBLANES:SUBLANES + tm, :] = ucv_ref[...]
    ext_scr[SUBLANES + tm:, :] = next_ref[...]
    tile_pos = (pl.program_id(0) % (lp // tm)) * tm
    rb = tm // OUT_ROW_BLOCKS
    for blk in range(OUT_ROW_BLOCKS):
        r0 = blk * rb
        rows = slice(r0, r0 + rb)
        pos = tile_pos + r0 + lax.broadcasted_iota(jnp.int32, (rb, 1), 0)
        left = jnp.where(pos == SEQ_PAD - N_META, 0.0, ext_scr[SUBLANES - 1 + r0:SUBLANES - 1 + r0 + rb, :])
        right = jnp.where(pos == lp - 1, 0.0, ext_scr[SUBLANES + 1 + r0:SUBLANES + 1 + r0 + rb, :])
        conv = (left * cw_ref[0:1, :] + ucv_ref[rows, :] * cw_ref[1:2, :] + right * cw_ref[2:3, :]
                + cbias_ref[...])
        y = cb_ref[rows, :] * conv
        for g in range(N_KV_HEADS):
            mix_scr[rows, g * GROUP_LANES:(g + 1) * GROUP_LANES] = ya_ref[g, rows, :]
        for half in range(CONV_WIDTH // MXU_WIDTH):
            cols = slice(half * MXU_WIDTH, (half + 1) * MXU_WIDTH)
            yh = y[:, cols]
            inv = lax.rsqrt(_segment_sums(yh * yh, seg_ref) * (1.0 / HEAD_DIM) + EPS)
            mix_scr[rows, ATTN_WIDTH + half * MXU_WIDTH:ATTN_WIDTH + (half + 1) * MXU_WIDTH] = (
                yh * inv * cgain_ref[:, cols]).astype(BF16)
        o_ref[rows, :] = h_ref[rows, :] + jnp.dot(mix_scr[rows, :], wo_ref[...], preferred_element_type=F32)


def _out_proj(ya, cb, ucv, h, conv_w, conv_b, conv_gain, w_out, seg, lp, tm):
    rows = h.shape[0]
    halo_per_tile = tm // SUBLANES
    last_halo = rows // SUBLANES - 1
    return pl.pallas_call(
        functools.partial(_out_body, tm=tm, lp=lp),
        grid=(rows // tm,),
        in_specs=[
            pl.BlockSpec((N_KV_HEADS, tm, GROUP_LANES), lambda i: (0, i, 0)),
            pl.BlockSpec((tm, CONV_WIDTH), lambda i: (i, 0)),
            pl.BlockSpec((tm, CONV_WIDTH), lambda i: (i, 0)),
            pl.BlockSpec((SUBLANES, CONV_WIDTH), lambda i: (jnp.maximum(i * halo_per_tile - 1, 0), 0)),
            pl.BlockSpec((SUBLANES, CONV_WIDTH), lambda i: (jnp.minimum((i + 1) * halo_per_tile, last_halo), 0)),
            pl.BlockSpec((tm, D_MODEL), lambda i: (i, 0)),
            _const_spec((3, CONV_WIDTH)),
            _const_spec((1, CONV_WIDTH)),
            _const_spec((1, CONV_WIDTH)),
            _const_spec((D_MODEL, D_MODEL)),
            _const_spec((2 * MXU_WIDTH, MXU_WIDTH)),
        ],
        out_specs=pl.BlockSpec((tm, D_MODEL), lambda i: (i, 0)),
        out_shape=jax.ShapeDtypeStruct((rows, D_MODEL), F32),
        scratch_shapes=[pltpu.VMEM((tm + 2 * SUBLANES, CONV_WIDTH), F32), pltpu.VMEM((tm, D_MODEL), BF16)],
        compiler_params=_params(1),
        name="out_proj",
    )(ya, cb, ucv, ucv, ucv, h, conv_w, conv_b, conv_gain, w_out, seg)


def _final_body(h_ref, g_ref, o_ref):
    o_ref[0] = _rms(h_ref[...]) * g_ref[...]


def _final_norm(h, gain, batch, n, tm):
    lp = SEQ_PAD + n
    tiles = n // tm
    return pl.pallas_call(
        _final_body,
        grid=(batch, tiles),
        in_specs=[
            pl.BlockSpec((pl.Element(tm), pl.Element(D_MODEL)), lambda b, i: (pl.multiple_of(b * lp + SEQ_PAD + i * tm, LANES), 0)),
            _const_spec((1, D_MODEL)),
        ],
        out_specs=pl.BlockSpec((1, tm, D_MODEL), lambda b, i: (b, i, 0)),
        out_shape=jax.ShapeDtypeStruct((batch, n, D_MODEL), F32),
        compiler_params=_params(2),
        name="final_norm",
    )(h, gain)


def _largest_tile(extent, cap):
    units = extent // LANES
    best = max(d for d in range(1, units + 1) if units % d == 0 and d * LANES <= cap)
    return best * LANES


def _in_column_order():
    halves_heads_pairs = [(half, r, i) for half in range(2) for r in range(KV_GROUP) for i in range(PAIRS)]
    cols = []
    for g in range(N_KV_HEADS):
        cols += [(KV_GROUP * g + r) * HEAD_DIM + 2 * i + half for half, r, i in halves_heads_pairs]
    for g in range(N_KV_HEADS):
        cols += [ATTN_WIDTH + g * HEAD_DIM + 2 * i + half for half, r, i in halves_heads_pairs]
    cols += list(range(ATTN_WIDTH + KV_WIDTH, IN_WIDTH))
    return np.asarray(cols, np.int32)


def _head_gain_order():
    return np.asarray([2 * i + half for half in range(2) for r in range(KV_GROUP) for i in range(PAIRS)], np.int32)


def _rope_tables(n):
    rows = n // GRID_W
    row = jnp.repeat(jnp.arange(rows, dtype=F32), GRID_W)
    col = jnp.tile(jnp.arange(GRID_W, dtype=F32), rows)
    row = jnp.concatenate([jnp.zeros((SEQ_PAD,), F32), row])
    col = jnp.concatenate([jnp.zeros((SEQ_PAD,), F32), col])
    freqs = ROPE_THETA ** (-jnp.arange(ROPE_PAIRS_AXIS, dtype=F32) / ROPE_PAIRS_AXIS)
    ang = jnp.concatenate([row[:, None] * freqs, col[:, None] * freqs], axis=-1)
    return jnp.tile(jnp.cos(ang), (1, KV_GROUP)), jnp.tile(jnp.sin(ang), (1, KV_GROUP))


def _stacked_block_diag_ones(size, block):
    idx = np.arange(size) // block
    seg = idx[:, None] == idx[None, :]
    return jnp.asarray(np.concatenate([seg, seg], axis=0), BF16)


def _trunk(x, meta_tokens, p):
    batch, n, d = x.shape
    assert d == D_MODEL and n % (2 * KV_CHUNK) == 0 and n % GRID_W == 0
    lp = SEQ_PAD + n
    rows = batch * lp
    tm_ffn = _largest_tile(rows, 1024)
    tm_seq = _largest_tile(lp, 768)
    tq = tm_seq
    h = jnp.concatenate([
        jnp.zeros((batch, SEQ_PAD - N_META, d), F32),
        jnp.broadcast_to(meta_tokens[None].astype(F32), (batch, N_META, d)),
        x.astype(F32)], axis=1).reshape(rows, d)
    cos, sin = _rope_tables(n)
    for l in range(p["depth"]):
        h = _ffn(h, p["ffn1_norm"][l], p["ffn1_w_gate"][l], p["ffn1_w_up"][l], p["ffn1_w_down"][l], tm_ffn)
        q, k, vt, cb, ucv = _in_proj(h, p["mix_norm"][l], p["w_in"][l], p["qk_gain"][l], cos, sin,
                                     p["seg_pairs"], lp, tm_seq)
        ya = _attention(q, k, vt, p["attn_gain"][l], batch, lp, tq)
        h = _out_proj(ya, cb, ucv, h, p["conv_w"][l], p["conv_b"][l], p["conv_gain"][l], p["w_out"][l],
                      p["seg_heads"], lp, tm_seq)
        h = _ffn(h, p["ffn2_norm"][l], p["ffn2_w_gate"][l], p["ffn2_w_up"][l], p["ffn2_w_down"][l], tm_ffn)
    return _final_norm(h, p["final_norm"], batch, n, 2 * KV_CHUNK)


def kernel(x_prompt, x_sample, meta_tokens, ffn1_norm, ffn1_w_gate, ffn1_w_up, ffn1_w_down, mix_norm, w_in, q_norm, k_norm, conv_w, conv_b, attn_out_norm, conv_out_norm, w_out, ffn2_norm, ffn2_w_gate, ffn2_w_up, ffn2_w_down, final_norm):
    depth = w_in.shape[0]
    order = _head_gain_order()
    q_gain = q_norm[:, order] * (HEAD_DIM ** -0.5 * LOG2_E)
    k_gain = k_norm[:, order]
    p = {
        "depth": depth,
        "ffn1_norm": ffn1_norm[:, None, :],
        "ffn1_w_gate": ffn1_w_gate.astype(BF16),
        "ffn1_w_up": ffn1_w_up.astype(BF16),
        "ffn1_w_down": ffn1_w_down.astype(BF16),
        "mix_norm": mix_norm[:, None, :],
        "w_in": w_in[:, :, _in_column_order()].astype(BF16),
        "qk_gain": jnp.stack([q_gain, q_gain, k_gain, k_gain], axis=1),
        "attn_gain": jnp.pad(attn_out_norm.reshape(depth, N_KV_HEADS, KV_GROUP, HEAD_DIM),
                             ((0, 0), (0, 0), (0, 0), (0, LANES - HEAD_DIM))).reshape(depth, N_KV_HEADS, KV_GROUP, LANES),
        "conv_w": conv_w,
        "conv_b": conv_b[:, None, :],
        "conv_gain": conv_out_norm[:, None, :],
        "w_out": w_out.astype(BF16),
        "ffn2_norm": ffn2_norm[:, None, :],
        "ffn2_w_gate": ffn2_w_gate.astype(BF16),
        "ffn2_w_up": ffn2_w_up.astype(BF16),
        "ffn2_w_down": ffn2_w_down.astype(BF16),
        "final_norm": final_norm[None, :],
        "seg_pairs": _stacked_block_diag_ones(HALF_LANES, PAIRS),
        "seg_heads": _stacked_block_diag_ones(MXU_WIDTH, HEAD_DIM),
    }
    return (_trunk(x_prompt, meta_tokens, p), _trunk(x_sample, meta_tokens, p))
```

```python
import functools

import numpy as np
import jax
import jax.numpy as jnp
from jax import lax
from jax.experimental import pallas as pl
from jax.experimental.pallas import tpu as pltpu

F32 = jnp.float32
BF16 = jnp.bfloat16

D_MODEL = 1024
N_META = 16
GRID_W = 64
HEAD_DIM = 64
N_Q_HEADS = 8
N_KV_HEADS = 2
KV_GROUP = N_Q_HEADS // N_KV_HEADS
ATTN_WIDTH = N_Q_HEADS * HEAD_DIM
KV_WIDTH = N_KV_HEADS * HEAD_DIM
CONV_WIDTH = D_MODEL - ATTN_WIDTH
D_FF = 2816
ROPE_THETA = 10000.0
ROPE_PAIRS_AXIS = HEAD_DIM // 4
EPS = 1e-6
IN_WIDTH = ATTN_WIDTH + 2 * KV_WIDTH + 3 * CONV_WIDTH

LANES = 128
SUBLANES = 8
MXU_WIDTH = 256
SEQ_PAD = LANES
GROUP_LANES = KV_GROUP * HEAD_DIM
HALF_LANES = GROUP_LANES // 2
PAIRS = HEAD_DIM // 2
QK_COLS = 2 * N_KV_HEADS * GROUP_LANES
IN_COLS = QK_COLS + KV_WIDTH + 3 * CONV_WIDTH
ONES_ROWS = 16
VT_ROWS = HEAD_DIM + ONES_ROWS
KV_CHUNK = 4 * LANES
LOOP_CHUNKS = 6
COL_BLOCK = MXU_WIDTH
OUT_ROW_BLOCKS = 4
FFN_ROW_BLOCKS = 4
FF_CHUNKS = ((0, 768), (768, 768), (1536, 768), (2304, 512))
MASKED_SCORE = -1e30
LOG2_E = 1.4426950408889634
VMEM_LIMIT = 56 * 1024 * 1024


def _const_spec(shape):
    zeros = (0,) * len(shape)
    return pl.BlockSpec(shape, lambda *_: zeros, pipeline_mode=pl.Buffered(1))


def _params(n_axes):
    return pltpu.CompilerParams(dimension_semantics=("arbitrary",) * n_axes, vmem_limit_bytes=VMEM_LIMIT)


def _rms(x):
    return x * lax.rsqrt(jnp.mean(x * x, axis=-1, keepdims=True) + EPS)


def _segment_sums(x, seg_ref):
    hi = x.astype(BF16)
    lo = (x - hi.astype(F32)).astype(BF16)
    return jnp.dot(jnp.concatenate([hi, lo], axis=1), seg_ref[...], preferred_element_type=F32)


def _ffn_body(h_ref, g_ref, wg_ref, wu_ref, wd_ref, o_ref, act_scr, *, tm):
    rb = tm // FFN_ROW_BLOCKS
    for blk in range(FFN_ROW_BLOCKS):
        rows = slice(blk * rb, (blk + 1) * rb)
        x = h_ref[rows, :]
        xn = (_rms(x) * g_ref[...]).astype(BF16)
        for start, size in FF_CHUNKS:
            gate = jnp.dot(xn, wg_ref[:, start:start + size], preferred_element_type=F32)
            up = jnp.dot(xn, wu_ref[:, start:start + size], preferred_element_type=F32)
            silu = gate * (1.0 / (1.0 + jnp.exp(-gate)))
            act_scr[rows, start:start + size] = (silu * up).astype(BF16)
        o_ref[rows, :] = x + 0.5 * jnp.dot(act_scr[rows, :], wd_ref[...], preferred_element_type=F32)


def _ffn(h, gain, w_gate, w_up, w_down, tm):
    rows = h.shape[0]
    return pl.pallas_call(
        functools.partial(_ffn_body, tm=tm),
        grid=(rows // tm,),
        in_specs=[
            pl.BlockSpec((tm, D_MODEL), lambda i: (i, 0)),
            _const_spec((1, D_MODEL)),
            _const_spec((D_MODEL, D_FF)),
            _const_spec((D_MODEL, D_FF)),
            _const_spec((D_FF, D_MODEL)),
        ],
        out_specs=pl.BlockSpec((tm, D_MODEL), lambda i: (i, 0)),
        out_shape=jax.ShapeDtypeStruct((rows, D_MODEL), F32),
        scratch_shapes=[pltpu.VMEM((tm, D_FF), BF16)],
        compiler_params=_params(1),
        name="ffn",
    )(h, gain, w_gate, w_up, w_down)


def _in_body(h_ref, g_ref, w_ref, qkg_ref, cos_ref, sin_ref, seg_ref,
             q_ref, k_ref, vt_ref, cb_ref, ucv_ref, *, tm):
    xn = (_rms(h_ref[...]) * g_ref[...]).astype(BF16)
    u_all = jnp.dot(xn, w_ref[...], preferred_element_type=F32)
    cos = cos_ref[...]
    sin = sin_ref[...]
    for c in range(2 * N_KV_HEADS):
        x0 = u_all[:, c * GROUP_LANES:c * GROUP_LANES + HALF_LANES]
        x1 = u_all[:, c * GROUP_LANES + HALF_LANES:(c + 1) * GROUP_LANES]
        ssq = _segment_sums(x0 * x0 + x1 * x1, seg_ref)
        inv = lax.rsqrt(ssq * (1.0 / HEAD_DIM) + EPS)
        a0 = x0 * inv * qkg_ref[c:c + 1, :HALF_LANES]
        a1 = x1 * inv * qkg_ref[c:c + 1, HALF_LANES:]
        o0 = a0 * cos - a1 * sin
        o1 = a0 * sin + a1 * cos
        out_ref = q_ref if c < N_KV_HEADS else k_ref
        out_ref[c % N_KV_HEADS, :, :HALF_LANES] = o0.astype(BF16)
        out_ref[c % N_KV_HEADS, :, HALF_LANES:] = o1.astype(BF16)
    v = u_all[:, QK_COLS:QK_COLS + KV_WIDTH]
    lane = lax.broadcasted_iota(jnp.int32, v.shape, 1)
    for g in range(N_KV_HEADS):
        v_ones = jnp.where(lane < HEAD_DIM, v if g == 0 else pltpu.roll(v, HEAD_DIM, 1), 1.0)
        for j in range(tm // LANES):
            vt_ref[g, j] = v_ones[j * LANES:(j + 1) * LANES, :].T[:VT_ROWS, :].astype(BF16)
    conv0 = QK_COLS + KV_WIDTH
    cb_ref[...] = u_all[:, conv0:conv0 + CONV_WIDTH]
    ucv_ref[...] = (u_all[:, conv0 + CONV_WIDTH:conv0 + 2 * CONV_WIDTH]
                    * u_all[:, conv0 + 2 * CONV_WIDTH:conv0 + 3 * CONV_WIDTH])


def _in_proj(h, gain, w_in, qk_gain, cos, sin, seg, lp, tm):
    rows = h.shape[0]
    tiles_per_seq = lp // tm
    return pl.pallas_call(
        functools.partial(_in_body, tm=tm),
        grid=(rows // tm,),
        in_specs=[
            pl.BlockSpec((tm, D_MODEL), lambda i: (i, 0)),
            _const_spec((1, D_MODEL)),
            _const_spec((D_MODEL, IN_COLS)),
            _const_spec((2 * N_KV_HEADS, GROUP_LANES)),
            pl.BlockSpec((tm, HALF_LANES), lambda i: (i % tiles_per_seq, 0)),
            pl.BlockSpec((tm, HALF_LANES), lambda i: (i % tiles_per_seq, 0)),
            _const_spec((2 * HALF_LANES, HALF_LANES)),
        ],
        out_specs=[
            pl.BlockSpec((N_KV_HEADS, tm, GROUP_LANES), lambda i: (0, i, 0)),
            pl.BlockSpec((N_KV_HEADS, tm, GROUP_LANES), lambda i: (0, i, 0)),
            pl.BlockSpec((N_KV_HEADS, tm // LANES, VT_ROWS, LANES), lambda i: (0, i, 0, 0)),
            pl.BlockSpec((tm, CONV_WIDTH), lambda i: (i, 0)),
            pl.BlockSpec((tm, CONV_WIDTH), lambda i: (i, 0)),
        ],
        out_shape=[
            jax.ShapeDtypeStruct((N_KV_HEADS, rows, GROUP_LANES), BF16),
            jax.ShapeDtypeStruct((N_KV_HEADS, rows, GROUP_LANES), BF16),
            jax.ShapeDtypeStruct((N_KV_HEADS, rows // LANES, VT_ROWS, LANES), BF16),
            jax.ShapeDtypeStruct((rows, CONV_WIDTH), F32),
            jax.ShapeDtypeStruct((rows, CONV_WIDTH), F32),
        ],
        compiler_params=_params(1),
        name="in_proj",
    )(h, gain, w_in, qk_gain, cos, sin, seg)


def _attn_body(q_ref, k_ref, vt_ref, gain_ref, o_ref, qt_scr, s_scr, mx_scr, m_scr, acc_scr, *, tq, n_steps):
    qt = q_ref[0].astype(F32).T
    dim = lax.broadcasted_iota(jnp.int32, qt.shape, 0)
    head_of_dim = (dim // PAIRS) % KV_GROUP
    for r in range(KV_GROUP):
        qt_scr[:, r * tq:(r + 1) * tq] = jnp.where(head_of_dim == r, qt, 0.0).astype(BF16)

    def sublane_tiles(x):
        return [x[j * SUBLANES:(j + 1) * SUBLANES, :] for j in range(x.shape[0] // SUBLANES)]

    col_blocks = [slice(b * COL_BLOCK, (b + 1) * COL_BLOCK) for b in range(KV_GROUP * tq // COL_BLOCK)]

    def produce(c, slot, cols):
        k = k_ref[0, pl.ds(pl.multiple_of(SEQ_PAD + c * KV_CHUNK, LANES), KV_CHUNK), :]
        s = jnp.dot(k, qt_scr[:, cols], preferred_element_type=F32)
        s_scr[slot, :, cols] = s
        mx_scr[slot, :, cols] = functools.reduce(jnp.maximum, sublane_tiles(s))

    def accumulate(s, m_cur, vt, cols):
        m_old = m_scr[:, cols]
        m_new = jnp.maximum(m_old, m_cur)
        p = jnp.exp2(s - m_new).astype(BF16)
        pv = jnp.dot(vt, p, preferred_element_type=F32)
        acc_scr[:, cols] = jnp.exp2(m_old - m_new) * acc_scr[:, cols] + pv
        m_scr[:, cols] = m_new

    def consume(c, slot, cols):
        vt4 = vt_ref[0, pl.ds(1 + c * (KV_CHUNK // LANES), KV_CHUNK // LANES)]
        vt = jnp.concatenate([vt4[j] for j in range(KV_CHUNK // LANES)], axis=1)
        accumulate(s_scr[slot, :, cols], jnp.max(mx_scr[slot, :, cols], axis=0, keepdims=True), vt, cols)

    def stage(produced, consumed):
        for cols in col_blocks:
            if produced is not None:
                produce(produced[0], produced[1], cols)
            if consumed is not None:
                consume(consumed[0], consumed[1], cols)

    m_scr[...] = jnp.full(m_scr.shape, MASKED_SCORE, F32)
    acc_scr[...] = jnp.zeros(acc_scr.shape, F32)

    stage((0, 0), None)

    per_trip = LOOP_CHUNKS if n_steps > 2 * LOOP_CHUNKS else 2

    def trip(i, carry):
        for u in range(per_trip):
            stage((i * per_trip + u + 1, (u + 1) % 2), (i * per_trip + u, u % 2))
        return carry

    looped = (n_steps - 2) // per_trip * per_trip
    lax.fori_loop(0, looped // per_trip, trip, 0)
    for c in range(looped, n_steps - 1):
        stage((c + 1, (c + 1) % 2), (c, c % 2))
    s = jnp.dot(k_ref[0, 0:SEQ_PAD, :], qt_scr[...], preferred_element_type=F32)
    key = lax.broadcasted_iota(jnp.int32, s.shape, 0)
    s = jnp.where(key >= SEQ_PAD - N_META, s, MASKED_SCORE)
    stage(None, (n_steps - 1, 1))
    for cols in col_blocks:
        accumulate(s[:, cols], jnp.max(s[:, cols], axis=0, keepdims=True), vt_ref[0, 0], cols)

    acc = acc_scr[...]
    is_value = lax.broadcasted_iota(jnp.int32, acc.shape, 0) < HEAD_DIM
    weight = jnp.where(is_value, 1.0 / HEAD_DIM, EPS / ONES_ROWS)
    yt = acc * lax.rsqrt(jnp.sum(acc * acc * weight, axis=0, keepdims=True))
    yt = jnp.concatenate([yt, jnp.zeros((LANES - VT_ROWS, yt.shape[1]), F32)], axis=0)
    heads = [yt[:, r * tq:(r + 1) * tq].T * gain_ref[0, r:r + 1, :] for r in range(KV_GROUP)]
    is_value = lax.broadcasted_iota(jnp.int32, (tq, LANES), 1) < HEAD_DIM
    o_ref[0, :, :LANES] = jnp.where(is_value, heads[0], pltpu.roll(heads[1], HEAD_DIM, 1)).astype(BF16)
    o_ref[0, :, LANES:] = jnp.where(is_value, heads[2], pltpu.roll(heads[3], HEAD_DIM, 1)).astype(BF16)


def _attention(q, k, vt, gain, batch, lp, tq):
    rows = batch * lp
    q_tiles = lp // tq
    chunks = lp // LANES
    n_steps = (lp - SEQ_PAD) // KV_CHUNK
    return pl.pallas_call(
        functools.partial(_attn_body, tq=tq, n_steps=n_steps),
        grid=(batch, N_KV_HEADS, q_tiles),
        in_specs=[
            pl.BlockSpec((1, tq, GROUP_LANES), lambda b, g, i: (g, b * q_tiles + i, 0)),
            pl.BlockSpec((1, lp, GROUP_LANES), lambda b, g, i: (g, b, 0)),
            pl.BlockSpec((1, chunks, VT_ROWS, LANES), lambda b, g, i: (g, b, 0, 0)),
            pl.BlockSpec((1, KV_GROUP, LANES), lambda b, g, i: (g, 0, 0)),
        ],
        out_specs=pl.BlockSpec((1, tq, GROUP_LANES), lambda b, g, i: (g, b * q_tiles + i, 0)),
        out_shape=jax.ShapeDtypeStruct((N_KV_HEADS, rows, GROUP_LANES), BF16),
        scratch_shapes=[
            pltpu.VMEM((GROUP_LANES, KV_GROUP * tq), BF16),
            pltpu.VMEM((2, KV_CHUNK, KV_GROUP * tq), F32),
            pltpu.VMEM((2, SUBLANES, KV_GROUP * tq), F32),
            pltpu.VMEM((1, KV_GROUP * tq), F32),
            pltpu.VMEM((VT_ROWS, KV_GROUP * tq), F32),
        ],
        compiler_params=_params(3),
        name="attn",
    )(q, k, vt, gain)


def _out_body(ya_ref, cb_ref, ucv_ref, prev_ref, next_ref, h_ref, cw_ref, cbias_ref, cgain_ref,
              wo_ref, seg_ref, o_ref, ext_scr, mix_scr, *, tm, lp):
    ext_scr[0:SUBLANES, :] = prev_ref[...]
    ext_scr[SUBLANES:SUBLANES + tm, :] = ucv_ref[...]
    ext_scr[SUBLANES + tm:, :] = next_ref[...]
    tile_pos = (pl.program_id(0) % (lp // tm)) * tm
    rb = tm // OUT_ROW_BLOCKS
    for blk in range(OUT_ROW_BLOCKS):
        r0 = blk * rb
        rows = slice(r0, r0 + rb)
        pos = tile_pos + r0 + lax.broadcasted_iota(jnp.int32, (rb, 1), 0)
        left = jnp.where(pos == SEQ_PAD - N_META, 0.0, ext_scr[SUBLANES - 1 + r0:SUBLANES - 1 + r0 + rb, :])
        right = jnp.where(pos == lp - 1, 0.0, ext_scr[SUBLANES + 1 + r0:SUBLANES + 1 + r0 + rb, :])
        conv = (left * cw_ref[0:1, :] + ucv_ref[rows, :] * cw_ref[1:2, :] + right * cw_ref[2:3, :]
                + cbias_ref[...])
        y = cb_ref[rows, :] * conv
        for g in range(N_KV_HEADS):
            mix_scr[rows, g * GROUP_LANES:(g + 1) * GROUP_LANES] = ya_ref[g, rows, :]
        for half in range(CONV_WIDTH // MXU_WIDTH):
            cols = slice(half * MXU_WIDTH, (half + 1) * MXU_WIDTH)
            yh = y[:, cols]
            inv = lax.rsqrt(_segment_sums(yh * yh, seg_ref) * (1.0 / HEAD_DIM) + EPS)
            mix_scr[rows, ATTN_WIDTH + half * MXU_WIDTH:ATTN_WIDTH + (half + 1) * MXU_WIDTH] = (
                yh * inv * cgain_ref[:, cols]).astype(BF16)
        o_ref[rows, :] = h_ref[rows, :] + jnp.dot(mix_scr[rows, :], wo_ref[...], preferred_element_type=F32)


def _out_proj(ya, cb, ucv, h, conv_w, conv_b, conv_gain, w_out, seg, lp, tm):
    rows = h.shape[0]
    halo_per_tile = tm // SUBLANES
    last_halo = rows // SUBLANES - 1
    return pl.pallas_call(
        functools.partial(_out_body, tm=tm, lp=lp),
        grid=(rows // tm,),
        in_specs=[
            pl.BlockSpec((N_KV_HEADS, tm, GROUP_LANES), lambda i: (0, i, 0)),
            pl.BlockSpec((tm, CONV_WIDTH), lambda i: (i, 0)),
            pl.BlockSpec((tm, CONV_WIDTH), lambda i: (i, 0)),
            pl.BlockSpec((SUBLANES, CONV_WIDTH), lambda i: (jnp.maximum(i * halo_per_tile - 1, 0), 0)),
            pl.BlockSpec((SUBLANES, CONV_WIDTH), lambda i: (jnp.minimum((i + 1) * halo_per_tile, last_halo), 0)),
            pl.BlockSpec((tm, D_MODEL), lambda i: (i, 0)),
            _const_spec((3, CONV_WIDTH)),
            _const_spec((1, CONV_WIDTH)),
            _const_spec((1, CONV_WIDTH)),
            _const_spec((D_MODEL, D_MODEL)),
            _const_spec((2 * MXU_WIDTH, MXU_WIDTH)),
        ],
        out_specs=pl.BlockSpec((tm, D_MODEL), lambda i: (i, 0)),
        out_shape=jax.ShapeDtypeStruct((rows, D_MODEL), F32),
        scratch_shapes=[pltpu.VMEM((tm + 2 * SUBLANES, CONV_WIDTH), F32), pltpu.VMEM((tm, D_MODEL), BF16)],
        compiler_params=_params(1),
        name="out_proj",
    )(ya, cb, ucv, ucv, ucv, h, conv_w, conv_b, conv_gain, w_out, seg)


def _final_body(h_ref, g_ref, o_ref):
    o_ref[0] = _rms(h_ref[...]) * g_ref[...]


def _final_norm(h, gain, batch, n, tm):
    lp = SEQ_PAD + n
    tiles = n // tm
    return pl.pallas_call(
        _final_body,
        grid=(batch, tiles),
        in_specs=[
            pl.BlockSpec((pl.Element(tm), pl.Element(D_MODEL)), lambda b, i: (pl.multiple_of(b * lp + SEQ_PAD + i * tm, LANES), 0)),
            _const_spec((1, D_MODEL)),
        ],
        out_specs=pl.BlockSpec((1, tm, D_MODEL), lambda b, i: (b, i, 0)),
        out_shape=jax.ShapeDtypeStruct((batch, n, D_MODEL), F32),
        compiler_params=_params(2),
        name="final_norm",
    )(h, gain)


def _largest_tile(extent, cap):
    units = extent // LANES
    best = max(d for d in range(1, units + 1) if units % d == 0 and d * LANES <= cap)
    return best * LANES


def _in_column_order():
    halves_heads_pairs = [(half, r, i) for half in range(2) for r in range(KV_GROUP) for i in range(PAIRS)]
    cols = []
    for g in range(N_KV_HEADS):
        cols += [(KV_GROUP * g + r) * HEAD_DIM + 2 * i + half for half, r, i in halves_heads_pairs]
    for g in range(N_KV_HEADS):
        cols += [ATTN_WIDTH + g * HEAD_DIM + 2 * i + half for half, r, i in halves_heads_pairs]
    cols += list(range(ATTN_WIDTH + KV_WIDTH, IN_WIDTH))
    return np.asarray(cols, np.int32)


def _head_gain_order():
    return np.asarray([2 * i + half for half in range(2) for r in range(KV_GROUP) for i in range(PAIRS)], np.int32)


def _rope_tables(n):
    rows = n // GRID_W
    row = jnp.repeat(jnp.arange(rows, dtype=F32), GRID_W)
    col = jnp.tile(jnp.arange(GRID_W, dtype=F32), rows)
    row = jnp.concatenate([jnp.zeros((SEQ_PAD,), F32), row])
    col = jnp.concatenate([jnp.zeros((SEQ_PAD,), F32), col])
    freqs = ROPE_THETA ** (-jnp.arange(ROPE_PAIRS_AXIS, dtype=F32) / ROPE_PAIRS_AXIS)
    ang = jnp.concatenate([row[:, None] * freqs, col[:, None] * freqs], axis=-1)
    return jnp.tile(jnp.cos(ang), (1, KV_GROUP)), jnp.tile(jnp.sin(ang), (1, KV_GROUP))


def _stacked_block_diag_ones(size, block):
    idx = np.arange(size) // block
    seg = idx[:, None] == idx[None, :]
    return jnp.asarray(np.concatenate([seg, seg], axis=0), BF16)


def _trunk(x, meta_tokens, p):
    batch, n, d = x.shape
    assert d == D_MODEL and n % (2 * KV_CHUNK) == 0 and n % GRID_W == 0
    lp = SEQ_PAD + n
    rows = batch * lp
    tm_ffn = _largest_tile(rows, 1024)
    tm_seq = _largest_tile(lp, 768)
    tq = tm_seq
    h = jnp.concatenate([
        jnp.zeros((batch, SEQ_PAD - N_META, d), F32),
        jnp.broadcast_to(meta_tokens[None].astype(F32), (batch, N_META, d)),
        x.astype(F32)], axis=1).reshape(rows, d)
    cos, sin = _rope_tables(n)
    for l in range(p["depth"]):
        h = _ffn(h, p["ffn1_norm"][l], p["ffn1_w_gate"][l], p["ffn1_w_up"][l], p["ffn1_w_down"][l], tm_ffn)
        q, k, vt, cb, ucv = _in_proj(h, p["mix_norm"][l], p["w_in"][l], p["qk_gain"][l], cos, sin,
                                     p["seg_pairs"], lp, tm_seq)
        ya = _attention(q, k, vt, p["attn_gain"][l], batch, lp, tq)
        h = _out_proj(ya, cb, ucv, h, p["conv_w"][l], p["conv_b"][l], p["conv_gain"][l], p["w_out"][l],
                      p["seg_heads"], lp, tm_seq)
        h = _ffn(h, p["ffn2_norm"][l], p["ffn2_w_gate"][l], p["ffn2_w_up"][l], p["ffn2_w_down"][l], tm_ffn)
    return _final_norm(h, p["final_norm"], batch, n, 2 * KV_CHUNK)


def kernel(x_prompt, x_sample, meta_tokens, ffn1_norm, ffn1_w_gate, ffn1_w_up, ffn1_w_down, mix_norm, w_in, q_norm, k_norm, conv_w, conv_b, attn_out_norm, conv_out_norm, w_out, ffn2_norm, ffn2_w_gate, ffn2_w_up, ffn2_w_down, final_norm):
    depth = w_in.shape[0]
    order = _head_gain_order()
    q_gain = q_norm[:, order] * (HEAD_DIM ** -0.5 * LOG2_E)
    k_gain = k_norm[:, order]
    p = {
        "depth": depth,
        "ffn1_norm": ffn1_norm[:, None, :],
        "ffn1_w_gate": ffn1_w_gate.astype(BF16),
        "ffn1_w_up": ffn1_w_up.astype(BF16),
        "ffn1_w_down": ffn1_w_down.astype(BF16),
        "mix_norm": mix_norm[:, None, :],
        "w_in": w_in[:, :, _in_column_order()].astype(BF16),
        "qk_gain": jnp.stack([q_gain, q_gain, k_gain, k_gain], axis=1),
        "attn_gain": jnp.pad(attn_out_norm.reshape(depth, N_KV_HEADS, KV_GROUP, HEAD_DIM),
                             ((0, 0), (0, 0), (0, 0), (0, LANES - HEAD_DIM))).reshape(depth, N_KV_HEADS, KV_GROUP, LANES),
        "conv_w": conv_w,
        "conv_b": conv_b[:, None, :],
        "conv_gain": conv_out_norm[:, None, :],
        "w_out": w_out.astype(BF16),
        "ffn2_norm": ffn2_norm[:, None, :],
        "ffn2_w_gate": ffn2_w_gate.astype(BF16),
        "ffn2_w_up": ffn2_w_up.astype(BF16),
        "ffn2_w_down": ffn2_w_down.astype(BF16),
        "final_norm": final_norm[None, :],
        "seg_pairs": _stacked_block_diag_ones(HALF_LANES, PAIRS),
        "seg_heads": _stacked_block_diag_ones(MXU_WIDTH, HEAD_DIM),
    }
    return (_trunk(x_prompt, meta_tokens, p), _trunk(x_sample, meta_tokens, p))
```

```python
import functools

import numpy as np
import jax
import jax.numpy as jnp
from jax import lax
from jax.experimental import pallas as pl
from jax.experimental.pallas import tpu as pltpu

F32 = jnp.float32
BF16 = jnp.bfloat16

D_MODEL = 1024
N_META = 16
GRID_W = 64
HEAD_DIM = 64
N_Q_HEADS = 8
N_KV_HEADS = 2
KV_GROUP = N_Q_HEADS // N_KV_HEADS
ATTN_WIDTH = N_Q_HEADS * HEAD_DIM
KV_WIDTH = N_KV_HEADS * HEAD_DIM
CONV_WIDTH = D_MODEL - ATTN_WIDTH
D_FF = 2816
ROPE_THETA = 10000.0
ROPE_PAIRS_AXIS = HEAD_DIM // 4
EPS = 1e-6
IN_WIDTH = ATTN_WIDTH + 2 * KV_WIDTH + 3 * CONV_WIDTH

LANES = 128
SUBLANES = 8
MXU_WIDTH = 256
SEQ_PAD = LANES
GROUP_LANES = KV_GROUP * HEAD_DIM
HALF_LANES = GROUP_LANES // 2
PAIRS = HEAD_DIM // 2
QK_COLS = 2 * N_KV_HEADS * GROUP_LANES
IN_COLS = QK_COLS + KV_WIDTH + 3 * CONV_WIDTH
ONES_ROWS = 16
VT_ROWS = HEAD_DIM + ONES_ROWS
KV_CHUNK = 4 * LANES
LOOP_CHUNKS = 6
COL_BLOCK = MXU_WIDTH
OUT_ROW_BLOCKS = 4
FFN_ROW_BLOCKS = 4
FF_CHUNKS = ((0, 768), (768, 768), (1536, 768), (2304, 512))
MASKED_SCORE = -1e30
LOG2_E = 1.4426950408889634
VMEM_LIMIT = 56 * 1024 * 1024


def _const_spec(shape):
    zeros = (0,) * len(shape)
    return pl.BlockSpec(shape, lambda *_: zeros, pipeline_mode=pl.Buffered(1))


def _params(n_axes):
    return pltpu.CompilerParams(dimension_semantics=("arbitrary",) * n_axes, vmem_limit_bytes=VMEM_LIMIT)


def _rms(x):
    return x * lax.rsqrt(jnp.mean(x * x, axis=-1, keepdims=True) + EPS)


def _segment_sums(x, seg_ref):
    hi = x.astype(BF16)
    lo = (x - hi.astype(F32)).astype(BF16)
    return jnp.dot(jnp.concatenate([hi, lo], axis=1), seg_ref[...], preferred_element_type=F32)


def _ffn_body(h_ref, g_ref, wg_ref, wu_ref, wd_ref, o_ref, act_scr, *, tm):
    rb = tm // FFN_ROW_BLOCKS
    for blk in range(FFN_ROW_BLOCKS):
        rows = slice(blk * rb, (blk + 1) * rb)
        x = h_ref[rows, :]
        xn = (_rms(x) * g_ref[...]).astype(BF16)
        for start, size in FF_CHUNKS:
            gate = jnp.dot(xn, wg_ref[:, start:start + size], preferred_element_type=F32)
            up = jnp.dot(xn, wu_ref[:, start:start + size], preferred_element_type=F32)
            silu = gate * (1.0 / (1.0 + jnp.exp(-gate)))
            act_scr[rows, start:start + size] = (silu * up).astype(BF16)
        o_ref[rows, :] = x + 0.5 * jnp.dot(act_scr[rows, :], wd_ref[...], preferred_element_type=F32)


def _ffn(h, gain, w_gate, w_up, w_down, tm):
    rows = h.shape[0]
    return pl.pallas_call(
        functools.partial(_ffn_body, tm=tm),
        grid=(rows // tm,),
        in_specs=[
            pl.BlockSpec((tm, D_MODEL), lambda i: (i, 0)),
            _const_spec((1, D_MODEL)),
            _const_spec((D_MODEL, D_FF)),
            _const_spec((D_MODEL, D_FF)),
            _const_spec((D_FF, D_MODEL)),
        ],
        out_specs=pl.BlockSpec((tm, D_MODEL), lambda i: (i, 0)),
        out_shape=jax.ShapeDtypeStruct((rows, D_MODEL), F32),
        scratch_shapes=[pltpu.VMEM((tm, D_FF), BF16)],
        compiler_params=_params(1),
        name="ffn",
    )(h, gain, w_gate, w_up, w_down)


def _in_body(h_ref, g_ref, w_ref, qkg_ref, cos_ref, sin_ref, seg_ref,
             q_ref, k_ref, vt_ref, cb_ref, ucv_ref, *, tm):
    xn = (_rms(h_ref[...]) * g_ref[...]).astype(BF16)
    u_all = jnp.dot(xn, w_ref[...], preferred_element_type=F32)
    cos = cos_ref[...]
    sin = sin_ref[...]
    for c in range(2 * N_KV_HEADS):
        x0 = u_all[:, c * GROUP_LANES:c * GROUP_LANES + HALF_LANES]
        x1 = u_all[:, c * GROUP_LANES + HALF_LANES:(c + 1) * GROUP_LANES]
        ssq = _segment_sums(x0 * x0 + x1 * x1, seg_ref)
        inv = lax.rsqrt(ssq * (1.0 / HEAD_DIM) + EPS)
        a0 = x0 * inv * qkg_ref[c:c + 1, :HALF_LANES]
        a1 = x1 * inv * qkg_ref[c:c + 1, HALF_LANES:]
        o0 = a0 * cos - a1 * sin
        o1 = a0 * sin + a1 * cos
        out_ref = q_ref if c < N_KV_HEADS else k_ref
        out_ref[c % N_KV_HEADS, :, :HALF_LANES] = o0.astype(BF16)
        out_ref[c % N_KV_HEADS, :, HALF_LANES:] = o1.astype(BF16)
    v = u_all[:, QK_COLS:QK_COLS + KV_WIDTH]
    lane = lax.broadcasted_iota(jnp.int32, v.shape, 1)
    for g in range(N_KV_HEADS):
        v_ones = jnp.where(lane < HEAD_DIM, v if g == 0 else pltpu.roll(v, HEAD_DIM, 1), 1.0)
        for j in range(tm // LANES):
            vt_ref[g, j] = v_ones[j * LANES:(j + 1) * LANES, :].T[:VT_ROWS, :].astype(BF16)
    conv0 = QK_COLS + KV_WIDTH
    cb_ref[...] = u_all[:, conv0:conv0 + CONV_WIDTH]
    ucv_ref[...] = (u_all[:, conv0 + CONV_WIDTH:conv0 + 2 * CONV_WIDTH]
                    * u_all[:, conv0 + 2 * CONV_WIDTH:conv0 + 3 * CONV_WIDTH])


def _in_proj(h, gain, w_in, qk_gain, cos, sin, seg, lp, tm):
    rows = h.shape[0]
    tiles_per_seq = lp // tm
    return pl.pallas_call(
        functools.partial(_in_body, tm=tm),
        grid=(rows // tm,),
        in_specs=[
            pl.BlockSpec((tm, D_MODEL), lambda i: (i, 0)),
            _const_spec((1, D_MODEL)),
            _const_spec((D_MODEL, IN_COLS)),
            _const_spec((2 * N_KV_HEADS, GROUP_LANES)),
            pl.BlockSpec((tm, HALF_LANES), lambda i: (i % tiles_per_seq, 0)),
            pl.BlockSpec((tm, HALF_LANES), lambda i: (i % tiles_per_seq, 0)),
            _const_spec((2 * HALF_LANES, HALF_LANES)),
        ],
        out_specs=[
            pl.BlockSpec((N_KV_HEADS, tm, GROUP_LANES), lambda i: (0, i, 0)),
            pl.BlockSpec((N_KV_HEADS, tm, GROUP_LANES), lambda i: (0, i, 0)),
            pl.BlockSpec((N_KV_HEADS, tm // LANES, VT_ROWS, LANES), lambda i: (0, i, 0, 0)),
            pl.BlockSpec((tm, CONV_WIDTH), lambda i: (i, 0)),
            pl.BlockSpec((tm, CONV_WIDTH), lambda i: (i, 0)),
        ],
        out_shape=[
            jax.ShapeDtypeStruct((N_KV_HEADS, rows, GROUP_LANES), BF16),
            jax.ShapeDtypeStruct((N_KV_HEADS, rows, GROUP_LANES), BF16),
            jax.ShapeDtypeStruct((N_KV_HEADS, rows // LANES, VT_ROWS, LANES), BF16),
            jax.ShapeDtypeStruct((rows, CONV_WIDTH), F32),
            jax.ShapeDtypeStruct((rows, CONV_WIDTH), F32),
        ],
        compiler_params=_params(1),
        name="in_proj",
    )(h, gain, w_in, qk_gain, cos, sin, seg)


def _attn_body(q_ref, k_ref, vt_ref, gain_ref, o_ref, qt_scr, s_scr, mx_scr, m_scr, acc_scr, *, tq, n_steps):
    qt = q_ref[0].astype(F32).T
    dim = lax.broadcasted_iota(jnp.int32, qt.shape, 0)
    head_of_dim = (dim // PAIRS) % KV_GROUP
    for r in range(KV_GROUP):
        qt_scr[:, r * tq:(r + 1) * tq] = jnp.where(head_of_dim == r, qt, 0.0).astype(BF16)

    def sublane_tiles(x):
        return [x[j * SUBLANES:(j + 1) * SUBLANES, :] for j in range(x.shape[0] // SUBLANES)]

    col_blocks = [slice(b * COL_BLOCK, (b + 1) * COL_BLOCK) for b in range(KV_GROUP * tq // COL_BLOCK)]

    def produce(c, slot, cols):
        k = k_ref[0, pl.ds(pl.multiple_of(SEQ_PAD + c * KV_CHUNK, LANES), KV_CHUNK), :]
        s = jnp.dot(k, qt_scr[:, cols], preferred_element_type=F32)
        s_scr[slot, :, cols] = s
        mx_scr[slot, :, cols] = functools.reduce(jnp.maximum, sublane_tiles(s))

    def accumulate(s, m_cur, vt, cols):
        m_old = m_scr[:, cols]
        m_new = jnp.maximum(m_old, m_cur)
        p = jnp.exp2(s - m_new).astype(BF16)
        pv = jnp.dot(vt, p, preferred_element_type=F32)
        acc_scr[:, cols] = jnp.exp2(m_old - m_new) * acc_scr[:, cols] + pv
        m_scr[:, cols] = m_new

    def consume(c, slot, cols):
        vt4 = vt_ref[0, pl.ds(1 + c * (KV_CHUNK // LANES), KV_CHUNK // LANES)]
        vt = jnp.concatenate([vt4[j] for j in range(KV_CHUNK // LANES)], axis=1)
        accumulate(s_scr[slot, :, cols], jnp.max(mx_scr[slot, :, cols], axis=0, keepdims=True), vt, cols)

    def stage(produced, consumed):
        for cols in col_blocks:
            if produced is not None:
                produce(produced[0], produced[1], cols)
            if consumed is not None:
                consume(consumed[0], consumed[1], cols)

    m_scr[...] = jnp.full(m_scr.shape, MASKED_SCORE, F32)
    acc_scr[...] = jnp.zeros(acc_scr.shape, F32)

    stage((0, 0), None)

    per_trip = LOOP_CHUNKS if n_steps > 2 * LOOP_CHUNKS else 2

    def trip(i, carry):
        for u in range(per_trip):
            stage((i * per_trip + u + 1, (u + 1) % 2), (i * per_trip + u, u % 2))
        return carry

    looped = (n_steps - 2) // per_trip * per_trip
    lax.fori_loop(0, looped // per_trip, trip, 0)
    for c in range(looped, n_steps - 1):
        stage((c + 1, (c + 1) % 2), (c, c % 2))
    s = jnp.dot(k_ref[0, 0:SEQ_PAD, :], qt_scr[...], preferred_element_type=F32)
    key = lax.broadcasted_iota(jnp.int32, s.shape, 0)
    s = jnp.where(key >= SEQ_PAD - N_META, s, MASKED_SCORE)
    stage(None, (n_steps - 1, 1))
    for cols in col_blocks:
        accumulate(s[:, cols], jnp.max(s[:, cols], axis=0, keepdims=True), vt_ref[0, 0], cols)

    acc = acc_scr[...]
    is_value = lax.broadcasted_iota(jnp.int32, acc.shape, 0) < HEAD_DIM
    weight = jnp.where(is_value, 1.0 / HEAD_DIM, EPS / ONES_ROWS)
    yt = acc * lax.rsqrt(jnp.sum(acc * acc * weight, axis=0, keepdims=True))
    yt = jnp.concatenate([yt, jnp.zeros((LANES - VT_ROWS, yt.shape[1]), F32)], axis=0)
    heads = [yt[:, r * tq:(r + 1) * tq].T * gain_ref[0, r:r + 1, :] for r in range(KV_GROUP)]
    is_value = lax.broadcasted_iota(jnp.int32, (tq, LANES), 1) < HEAD_DIM
    o_ref[0, :, :LANES] = jnp.where(is_value, heads[0], pltpu.roll(heads[1], HEAD_DIM, 1)).astype(BF16)
    o_ref[0, :, LANES:] = jnp.where(is_value, heads[2], pltpu.roll(heads[3], HEAD_DIM, 1)).astype(BF16)


def _attention(q, k, vt, gain, batch, lp, tq):
    rows = batch * lp
    q_tiles = lp // tq
    chunks = lp // LANES
    n_steps = (lp - SEQ_PAD) // KV_CHUNK
    return pl.pallas_call(
        functools.partial(_attn_body, tq=tq, n_steps=n_steps),
        grid=(batch, N_KV_HEADS, q_tiles),
        in_specs=[
            pl.BlockSpec((1, tq, GROUP_LANES), lambda b, g, i: (g, b * q_tiles + i, 0)),
            pl.BlockSpec((1, lp, GROUP_LANES), lambda b, g, i: (g, b, 0), pipeline_mode=pl.Buffered(1)),
            pl.BlockSpec((1, chunks, VT_ROWS, LANES), lambda b, g, i: (g, b, 0, 0), pipeline_mode=pl.Buffered(1)),
            pl.BlockSpec((1, KV_GROUP, LANES), lambda b, g, i: (g, 0, 0)),
        ],
        out_specs=pl.BlockSpec((1, tq, GROUP_LANES), lambda b, g, i: (g, b * q_tiles + i, 0)),
        out_shape=jax.ShapeDtypeStruct((N_KV_HEADS, rows, GROUP_LANES), BF16),
        scratch_shapes=[
            pltpu.VMEM((GROUP_LANES, KV_GROUP * tq), BF16),
            pltpu.VMEM((2, KV_CHUNK, KV_GROUP * tq + LANES), F32),
            pltpu.VMEM((2, SUBLANES, KV_GROUP * tq), F32),
            pltpu.VMEM((1, KV_GROUP * tq), F32),
            pltpu.VMEM((VT_ROWS, KV_GROUP * tq), F32),
        ],
        compiler_params=_params(3),
        name="attn",
    )(q, k, vt, gain)


def _out_body(ya_ref, cb_ref, ucv_ref, prev_ref, next_ref, h_ref, cw_ref, cbias_ref, cgain_ref,
              wo_ref, seg_ref, o_ref, ext_scr, mix_scr, *, tm, lp):
    ext_scr[0:SUBLANES, :] = prev_ref[...]
    ext_scr[SUBLANES:SUBLANES + tm, :] = ucv_ref[...]
    ext_scr[SUBLANES + tm:, :] = next_ref[...]
    tile_pos = (pl.program_id(0) % (lp // tm)) * tm
    rb = tm // OUT_ROW_BLOCKS
    for blk in range(OUT_ROW_BLOCKS):
        r0 = blk * rb
        rows = slice(r0, r0 + rb)
        pos = tile_pos + r0 + lax.broadcasted_iota(jnp.int32, (rb, 1), 0)
        left = jnp.where(pos == SEQ_PAD - N_META, 0.0, ext_scr[SUBLANES - 1 + r0:SUBLANES - 1 + r0 + rb, :])
        right = jnp.where(pos == lp - 1, 0.0, ext_scr[SUBLANES + 1 + r0:SUBLANES + 1 + r0 + rb, :])
        conv = (left * cw_ref[0:1, :] + ucv_ref[rows, :] * cw_ref[1:2, :] + right * cw_ref[2:3, :]
                + cbias_ref[...])
        y = cb_ref[rows, :] * conv
        for g in range(N_KV_HEADS):
            mix_scr[rows, g * GROUP_LANES:(g + 1) * GROUP_LANES] = ya_ref[g, rows, :]
        for half in range(CONV_WIDTH // MXU_WIDTH):
            cols = slice(half * MXU_WIDTH, (half + 1) * MXU_WIDTH)
            yh = y[:, cols]
            inv = lax.rsqrt(_segment_sums(yh * yh, seg_ref) * (1.0 / HEAD_DIM) + EPS)
            mix_scr[rows, ATTN_WIDTH + half * MXU_WIDTH:ATTN_WIDTH + (half + 1) * MXU_WIDTH] = (
                yh * inv * cgain_ref[:, cols]).astype(BF16)
        o_ref[rows, :] = h_ref[rows, :] + jnp.dot(mix_scr[rows, :], wo_ref[...], preferred_element_type=F32)


def _out_proj(ya, cb, ucv, h, conv_w, conv_b, conv_gain, w_out, seg, lp, tm):
    rows = h.shape[0]
    halo_per_tile = tm // SUBLANES
    last_halo = rows // SUBLANES - 1
    return pl.pallas_call(
        functools.partial(_out_body, tm=tm, lp=lp),
        grid=(rows // tm,),
        in_specs=[
            pl.BlockSpec((N_KV_HEADS, tm, GROUP_LANES), lambda i: (0, i, 0)),
            pl.BlockSpec((tm, CONV_WIDTH), lambda i: (i, 0)),
            pl.BlockSpec((tm, CONV_WIDTH), lambda i: (i, 0)),
            pl.BlockSpec((SUBLANES, CONV_WIDTH), lambda i: (jnp.maximum(i * halo_per_tile - 1, 0), 0)),
            pl.BlockSpec((SUBLANES, CONV_WIDTH), lambda i: (jnp.minimum((i + 1) * halo_per_tile, last_halo), 0)),
            pl.BlockSpec((tm, D_MODEL), lambda i: (i, 0)),
            _const_spec((3, CONV_WIDTH)),
            _const_spec((1, CONV_WIDTH)),
            _const_spec((1, CONV_WIDTH)),
            _const_spec((D_MODEL, D_MODEL)),
            _const_spec((2 * MXU_WIDTH, MXU_WIDTH)),
        ],
        out_specs=pl.BlockSpec((tm, D_MODEL), lambda i: (i, 0)),
        out_shape=jax.ShapeDtypeStruct((rows, D_MODEL), F32),
        scratch_shapes=[pltpu.VMEM((tm + 2 * SUBLANES, CONV_WIDTH), F32), pltpu.VMEM((tm, D_MODEL), BF16)],
        compiler_params=_params(1),
        name="out_proj",
    )(ya, cb, ucv, ucv, ucv, h, conv_w, conv_b, conv_gain, w_out, seg)


def _final_body(h_ref, g_ref, o_ref):
    o_ref[0] = _rms(h_ref[...]) * g_ref[...]


def _final_norm(h, gain, batch, n, tm):
    lp = SEQ_PAD + n
    tiles = n // tm
    return pl.pallas_call(
        _final_body,
        grid=(batch, tiles),
        in_specs=[
            pl.BlockSpec((pl.Element(tm), pl.Element(D_MODEL)), lambda b, i: (pl.multiple_of(b * lp + SEQ_PAD + i * tm, LANES), 0)),
            _const_spec((1, D_MODEL)),
        ],
        out_specs=pl.BlockSpec((1, tm, D_MODEL), lambda b, i: (b, i, 0)),
        out_shape=jax.ShapeDtypeStruct((batch, n, D_MODEL), F32),
        compiler_params=_params(2),
        name="final_norm",
    )(h, gain)


def _largest_tile(extent, cap):
    units = extent // LANES
    best = max(d for d in range(1, units + 1) if units % d == 0 and d * LANES <= cap)
    return best * LANES


def _in_column_order():
    halves_heads_pairs = [(half, r, i) for half in range(2) for r in range(KV_GROUP) for i in range(PAIRS)]
    cols = []
    for g in range(N_KV_HEADS):
        cols += [(KV_GROUP * g + r) * HEAD_DIM + 2 * i + half for half, r, i in halves_heads_pairs]
    for g in range(N_KV_HEADS):
        cols += [ATTN_WIDTH + g * HEAD_DIM + 2 * i + half for half, r, i in halves_heads_pairs]
    cols += list(range(ATTN_WIDTH + KV_WIDTH, IN_WIDTH))
    return np.asarray(cols, np.int32)


def _head_gain_order():
    return np.asarray([2 * i + half for half in range(2) for r in range(KV_GROUP) for i in range(PAIRS)], np.int32)


def _rope_tables(n):
    rows = n // GRID_W
    row = jnp.repeat(jnp.arange(rows, dtype=F32), GRID_W)
    col = jnp.tile(jnp.arange(GRID_W, dtype=F32), rows)
    row = jnp.concatenate([jnp.zeros((SEQ_PAD,), F32), row])
    col = jnp.concatenate([jnp.zeros((SEQ_PAD,), F32), col])
    freqs = ROPE_THETA ** (-jnp.arange(ROPE_PAIRS_AXIS, dtype=F32) / ROPE_PAIRS_AXIS)
    ang = jnp.concatenate([row[:, None] * freqs, col[:, None] * freqs], axis=-1)
    return jnp.tile(jnp.cos(ang), (1, KV_GROUP)), jnp.tile(jnp.sin(ang), (1, KV_GROUP))


def _stacked_block_diag_ones(size, block):
    idx = np.arange(size) // block
    seg = idx[:, None] == idx[None, :]
    return jnp.asarray(np.concatenate([seg, seg], axis=0), BF16)


def _trunk(x, meta_tokens, p):
    batch, n, d = x.shape
    assert d == D_MODEL and n % (2 * KV_CHUNK) == 0 and n % GRID_W == 0
    lp = SEQ_PAD + n
    rows = batch * lp
    tm_ffn = _largest_tile(rows, 1024)
    tm_seq = _largest_tile(lp, 768)
    tq = tm_seq
    h = jnp.concatenate([
        jnp.zeros((batch, SEQ_PAD - N_META, d), F32),
        jnp.broadcast_to(meta_tokens[None].astype(F32), (batch, N_META, d)),
        x.astype(F32)], axis=1).reshape(rows, d)
    cos, sin = _rope_tables(n)
    for l in range(p["depth"]):
        h = _ffn(h, p["ffn1_norm"][l], p["ffn1_w_gate"][l], p["ffn1_w_up"][l], p["ffn1_w_down"][l], tm_ffn)
        q, k, vt, cb, ucv = _in_proj(h, p["mix_norm"][l], p["w_in"][l], p["qk_gain"][l], cos, sin,
                                     p["seg_pairs"], lp, tm_seq)
        ya = _attention(q, k, vt, p["attn_gain"][l], batch, lp, tq)
        h = _out_proj(ya, cb, ucv, h, p["conv_w"][l], p["conv_b"][l], p["conv_gain"][l], p["w_out"][l],
                      p["seg_heads"], lp, tm_seq)
        h = _ffn(h, p["ffn2_norm"][l], p["ffn2_w_gate"][l], p["ffn2_w_up"][l], p["ffn2_w_down"][l], tm_ffn)
    return _final_norm(h, p["final_norm"], batch, n, 2 * KV_CHUNK)


def kernel(x_prompt, x_sample, meta_tokens, ffn1_norm, ffn1_w_gate, ffn1_w_up, ffn1_w_down, mix_norm, w_in, q_norm, k_norm, conv_w, conv_b, attn_out_norm, conv_out_norm, w_out, ffn2_norm, ffn2_w_gate, ffn2_w_up, ffn2_w_down, final_norm):
    depth = w_in.shape[0]
    order = _head_gain_order()
    q_gain = q_norm[:, order] * (HEAD_DIM ** -0.5 * LOG2_E)
    k_gain = k_norm[:, order]
    p = {
        "depth": depth,
        "ffn1_norm": ffn1_norm[:, None, :],
        "ffn1_w_gate": ffn1_w_gate.astype(BF16),
        "ffn1_w_up": ffn1_w_up.astype(BF16),
        "ffn1_w_down": ffn1_w_down.astype(BF16),
        "mix_norm": mix_norm[:, None, :],
        "w_in": w_in[:, :, _in_column_order()].astype(BF16),
        "qk_gain": jnp.stack([q_gain, q_gain, k_gain, k_gain], axis=1),
        "attn_gain": jnp.pad(attn_out_norm.reshape(depth, N_KV_HEADS, KV_GROUP, HEAD_DIM),
                             ((0, 0), (0, 0), (0, 0), (0, LANES - HEAD_DIM))).reshape(depth, N_KV_HEADS, KV_GROUP, LANES),
        "conv_w": conv_w,
        "conv_b": conv_b[:, None, :],
        "conv_gain": conv_out_norm[:, None, :],
        "w_out": w_out.astype(BF16),
        "ffn2_norm": ffn2_norm[:, None, :],
        "ffn2_w_gate": ffn2_w_gate.astype(BF16),
        "ffn2_w_up": ffn2_w_up.astype(BF16),
        "ffn2_w_down": ffn2_w_down.astype(BF16),
        "final_norm": final_norm[None, :],
        "seg_pairs": _stacked_block_diag_ones(HALF_LANES, PAIRS),
        "seg_heads": _stacked_block_diag_ones(MXU_WIDTH, HEAD_DIM),
    }
    return (_trunk(x_prompt, meta_tokens, p), _trunk(x_sample, meta_tokens, p))
```
